```python
import jax, jax.numpy as jnp
from jax import lax
import numpy as np

D_MODEL = 4096
BATCH = 4
SEQ = 4096
DEPTH = 1

N_META = 16
CONV_DIM = D_MODEL
CONV_WIDTH = 31
SSD_INNER = 2 * D_MODEL
SSD_HEAD_DIM = 64
SSD_HEADS = SSD_INNER // SSD_HEAD_DIM
SSD_GROUPS = 8
SSD_HEADS_PER_GROUP = SSD_HEADS // SSD_GROUPS
SSD_STATE = 128
SSD_CONV_WIDTH = 4
SSD_CONV_DIM = SSD_INNER + 2 * SSD_GROUPS * SSD_STATE
CHUNK = 128
N_BRANCH = 2
IN_SPLIT_SIZES = (CONV_DIM, CONV_DIM, CONV_DIM,
                  SSD_INNER, SSD_CONV_DIM, SSD_HEADS,
                  N_BRANCH * D_MODEL)
IN_COLS = sum(IN_SPLIT_SIZES)
LN_EPS = 1e-5
RMS_EPS = 1e-5
DEEPNORM_ALPHA = (2.0 * DEPTH) ** 0.25
DEEPNORM_BETA = (8.0 * DEPTH) ** -0.25

kernel_name = "conformer_ssd_gated_hybrid_deepnorm"


def _layer_norm(x, g, b):
    xf = x.astype(jnp.float32)
    mu = jnp.mean(xf, axis=-1, keepdims=True)
    var = jnp.mean(jnp.square(xf - mu), axis=-1, keepdims=True)
    y = (xf - mu) * lax.rsqrt(var + LN_EPS)
    return (y * g.astype(jnp.float32) + b.astype(jnp.float32)).astype(x.dtype)


def _gated_group_rmsnorm(y, z, g):
    b, l, c = y.shape
    yz = (y * jax.nn.silu(z)).astype(jnp.float32).reshape(b, l, SSD_GROUPS, c // SSD_GROUPS)
    yz = yz * lax.rsqrt(jnp.mean(jnp.square(yz), axis=-1, keepdims=True) + RMS_EPS)
    return (yz.reshape(b, l, c) * g.astype(jnp.float32)).astype(y.dtype)


def _causal_depthwise_conv(x, w, bias):
    k = w.shape[0]
    out = lax.conv_general_dilated(
        x, w[:, None, :].astype(x.dtype), window_strides=(1,), padding=[(k - 1, 0)],
        dimension_numbers=('NWC', 'WIO', 'NWC'), feature_group_count=x.shape[-1])
    return out + bias


def _segsum_exp(a_cs):
    q = a_cs.shape[-1]
    diff = a_cs[..., :, None] - a_cs[..., None, :]
    mask = jnp.tril(jnp.ones((q, q), dtype=bool))
    return jnp.exp(jnp.where(mask, diff, -jnp.inf))


def _ssd_chunked(xh, dt, a, bm, cm):
    b, l, h, p = xh.shape
    pad = CHUNK - N_META
    t = l + pad
    nc = t // CHUNK
    g, r, n = SSD_GROUPS, SSD_HEADS_PER_GROUP, SSD_STATE
    xh = xh.astype(jnp.float32)
    xdt = jnp.pad(xh * dt[..., None], ((0, 0), (pad, 0), (0, 0), (0, 0))).reshape(b, nc, CHUNK, g, r, p)
    da = jnp.pad(dt * a, ((0, 0), (pad, 0), (0, 0))).reshape(b, nc, CHUNK, g, r).transpose(0, 3, 4, 1, 2)
    bc = jnp.pad(bm.astype(jnp.float32), ((0, 0), (pad, 0), (0, 0), (0, 0))).reshape(b, nc, CHUNK, g, n)
    cc = jnp.pad(cm.astype(jnp.float32), ((0, 0), (pad, 0), (0, 0), (0, 0))).reshape(b, nc, CHUNK, g, n)
    a_cs = jnp.cumsum(da, axis=-1)
    decay_mat = _segsum_exp(a_cs)
    cb = jnp.einsum('bclgn,bcsgn->bgcls', cc, bc)
    y_diag = jnp.einsum('bgcls,bgrcls,bcsgrp->bclgrp', cb, decay_mat, xdt)
    decay_states = jnp.exp(a_cs[..., -1:] - a_cs)
    states = jnp.einsum('bclgn,bgrcl,bclgrp->cbgrpn', bc, decay_states, xdt)
    chunk_decay = jnp.moveaxis(jnp.exp(a_cs[..., -1]), 3, 0)

    def step(hstate, inp):
        s_c, d_c = inp
        return hstate * d_c[..., None, None] + s_c, hstate

    h0 = jnp.zeros((b, g, r, p, n), jnp.float32)
    _, h_prev = lax.scan(step, h0, (states, chunk_decay))
    y_off = jnp.einsum('bclgn,cbgrpn,bgrcl->bclgrp', cc, h_prev, jnp.exp(a_cs))
    y = (y_diag + y_off).reshape(b, t, h, p)
    return y[:, pad:]


def setup_inputs(seed: int = 0) -> dict:
    key = jax.random.key(seed)
    ks = jax.random.split(key, 24)
    f32 = jnp.float32
    nrm = lambda k, s, sc: jax.random.normal(k, s, f32) * sc
    dt0 = jnp.exp(jax.random.uniform(ks[12], (DEPTH, SSD_HEADS), f32, np.log(1e-3), np.log(1e-1)))
    return {
        "x": nrm(ks[0], (BATCH, SEQ, D_MODEL), 1.0),
        "meta_tokens": nrm(ks[1], (N_META, D_MODEL), 1.0),
        "ln_in_g": 1.0 + nrm(ks[2], (D_MODEL,), 0.02),
        "ln_in_b": nrm(ks[3], (D_MODEL,), 0.02),
        "w_in": nrm(ks[4], (DEPTH, D_MODEL, IN_COLS), D_MODEL ** -0.5),
        "b_gate": nrm(ks[5], (DEPTH, N_BRANCH * D_MODEL), 0.02),
        "conv_w": nrm(ks[6], (DEPTH, CONV_WIDTH, CONV_DIM), CONV_WIDTH ** -0.5),
        "conv_b": nrm(ks[7], (DEPTH, CONV_DIM), 0.02),
        "conv_ln_g": 1.0 + nrm(ks[8], (DEPTH, CONV_DIM), 0.02),
        "conv_ln_b": nrm(ks[9], (DEPTH, CONV_DIM), 0.02),
        "w_conv_out": nrm(ks[10], (DEPTH, CONV_DIM, D_MODEL), DEEPNORM_BETA * CONV_DIM ** -0.5),
        "ssd_conv_w": nrm(ks[11], (DEPTH, SSD_CONV_WIDTH, SSD_CONV_DIM), SSD_CONV_WIDTH ** -0.5),
        "ssd_conv_b": nrm(ks[13], (DEPTH, SSD_CONV_DIM), 0.02),
        "dt_bias": dt0 + jnp.log(-jnp.expm1(-dt0)),
        "a_log": jnp.log(jax.random.uniform(ks[14], (DEPTH, SSD_HEADS), f32, 1.0, 16.0)),
        "d_skip": 1.0 + nrm(ks[15], (DEPTH, SSD_HEADS), 0.02),
        "ssd_norm_g": 1.0 + nrm(ks[16], (DEPTH, SSD_INNER), 0.02),
        "w_ssd_out": nrm(ks[17], (DEPTH, SSD_INNER, D_MODEL), DEEPNORM_BETA * SSD_INNER ** -0.5),
        "w_out": nrm(ks[18], (DEPTH, D_MODEL, D_MODEL), DEEPNORM_BETA * D_MODEL ** -0.5),
        "ln_out_g": 1.0 + nrm(ks[19], (DEPTH, D_MODEL), 0.02),
        "ln_out_b": nrm(ks[20], (DEPTH, D_MODEL), 0.02),
    }


def reference(x, meta_tokens, ln_in_g, ln_in_b, w_in, b_gate, conv_w, conv_b, conv_ln_g, conv_ln_b,
              w_conv_out, ssd_conv_w, ssd_conv_b, dt_bias, a_log, d_skip, ssd_norm_g, w_ssd_out,
              w_out, ln_out_g, ln_out_b):
    b = x.shape[0]
    meta = jnp.broadcast_to(meta_tokens[None].astype(x.dtype), (b, N_META, D_MODEL))
    h = _layer_norm(jnp.concatenate([meta, x], axis=1), ln_in_g, ln_in_b)
    l = h.shape[1]
    split_at = [int(s) for s in np.cumsum(IN_SPLIT_SIZES)[:-1]]
    for i in range(DEPTH):
        proj = jnp.einsum('bld,de->ble', h, w_in[i])
        c_val, c_glu, c_gate, z, xbc, dt_raw, gates = jnp.split(proj, split_at, axis=-1)
        v = c_val * jax.nn.sigmoid(c_glu)
        v = _causal_depthwise_conv(v, conv_w[i], conv_b[i])
        v = jax.nn.silu(_layer_norm(v, conv_ln_g[i], conv_ln_b[i])) * jax.nn.silu(c_gate)
        y_conv = jnp.einsum('blc,cd->bld', v, w_conv_out[i])
        xbc = jax.nn.silu(_causal_depthwise_conv(xbc, ssd_conv_w[i], ssd_conv_b[i]))
        xs, bm, cm = jnp.split(xbc, [SSD_INNER, SSD_INNER + SSD_GROUPS * SSD_STATE], axis=-1)
        xh = xs.reshape(b, l, SSD_HEADS, SSD_HEAD_DIM)
        bm = bm.reshape(b, l, SSD_GROUPS, SSD_STATE)
        cm = cm.reshape(b, l, SSD_GROUPS, SSD_STATE)
        dt = jax.nn.softplus(dt_raw.astype(jnp.float32) + dt_bias[i].astype(jnp.float32))
        a = -jnp.exp(a_log[i].astype(jnp.float32))
        y = _ssd_chunked(xh, dt, a, bm, cm) + d_skip[i].astype(jnp.float32)[:, None] * xh.astype(jnp.float32)
        y = _gated_group_rmsnorm(y.reshape(b, l, SSD_INNER).astype(h.dtype), z, ssd_norm_g[i])
        y_ssd = jnp.einsum('blc,cd->bld', y, w_ssd_out[i])
        g_conv, g_ssd = jnp.split(jax.nn.sigmoid(gates + b_gate[i]), N_BRANCH, axis=-1)
        sub = jnp.einsum('bld,de->ble', g_conv * y_conv + g_ssd * y_ssd, w_out[i])
        h = _layer_norm(DEEPNORM_ALPHA * h + sub, ln_out_g[i], ln_out_b[i])
    return h[:, N_META:]
```

```python
import functools

import jax
import jax.numpy as jnp
from jax import lax
from jax.experimental import pallas as pl
from jax.experimental.pallas import tpu as pltpu

F32 = jnp.float32
BF16 = jnp.bfloat16

N_META = 16
CHUNK = 128
PAD = CHUNK - N_META
CONV_WIDTH = 31
SSD_HEAD_DIM = 64
SSD_GROUPS = 8
SSD_STATE = 128
SSD_CONV_WIDTH = 4
LN_EPS = 1e-5
RMS_EPS = 1e-5
DEEPNORM_ALPHA = 2.0 ** 0.25

SUBLANES = 8
HALO = 32
VMEM_LIMIT = 56 * 1024 * 1024


def _sigmoid(x):
    return 1.0 / (1.0 + jnp.exp(-x))


def _silu(x):
    return x * _sigmoid(x)


def _softplus(x):
    return jnp.maximum(x, 0.0) + jnp.log1p(jnp.exp(-jnp.abs(x)))


def _layer_norm_rows(v, g, b):
    mu = jnp.mean(v, axis=-1, keepdims=True)
    d = v - mu
    var = jnp.mean(d * d, axis=-1, keepdims=True)
    return d * lax.rsqrt(var + LN_EPS) * g + b


def _ln_in_kernel(x_ref, meta_ref, g_ref, b_ref, o_ref):
    t = pl.program_id(1)
    g = g_ref[...]
    b = b_ref[...]

    @pl.when(t == 0)
    def _():
        d = o_ref.shape[-1]
        o_ref[0, :PAD, :] = jnp.zeros((PAD, d), o_ref.dtype)
        o_ref[0, PAD:, :] = _layer_norm_rows(meta_ref[...], g, b).astype(o_ref.dtype)

    @pl.when(t > 0)
    def _():
        o_ref[0] = _layer_norm_rows(x_ref[0], g, b).astype(o_ref.dtype)


def _ln_in(x, meta, g, b):
    nb, seq, d = x.shape
    nt = seq // CHUNK + 1
    return pl.pallas_call(
        _ln_in_kernel,
        grid=(nb, nt),
        in_specs=[
            pl.BlockSpec((1, CHUNK, d), lambda bi, t: (bi, jnp.maximum(t - 1, 0), 0)),
            pl.BlockSpec((N_META, d), lambda bi, t: (0, 0)),
            pl.BlockSpec((1, d), lambda bi, t: (0, 0)),
            pl.BlockSpec((1, d), lambda bi, t: (0, 0)),
        ],
        out_specs=pl.BlockSpec((1, CHUNK, d), lambda bi, t: (bi, t, 0)),
        out_shape=jax.ShapeDtypeStruct((nb, nt * CHUNK, d), BF16),
        name="ln_in",
    )(x, meta, g.reshape(1, d), b.reshape(1, d))


def _mm_kernel(*refs, n_w, epilogue):
    x_ref = refs[0]
    w_refs = refs[1:1 + n_w]
    e_refs = refs[1 + n_w:-1]
    o_ref = refs[-1]
    x = x_ref[...]
    accs = [jnp.dot(x, w[...], preferred_element_type=F32) for w in w_refs]
    o_ref[...] = epilogue(accs, [e[...] for e in e_refs]).astype(o_ref.dtype)


def _matmul(x, ws, extras, epilogue, out_dtype, tm, tn, name):
    r, k = x.shape
    n = ws[0].shape[1]
    in_specs = [pl.BlockSpec((tm, k), lambda j, i: (i, 0))]
    in_specs += [pl.BlockSpec((k, tn), lambda j, i: (0, j)) for _ in ws]
    args = [x] + list(ws)
    for arr, off in extras:
        if arr.shape[0] == 1:
            in_specs.append(pl.BlockSpec((1, tn), lambda j, i, off=off: (0, j + off)))
        else:
            in_specs.append(pl.BlockSpec((tm, tn), lambda j, i, off=off: (i, j + off)))
        args.append(arr)
    return pl.pallas_call(
        functools.partial(_mm_kernel, n_w=len(ws), epilogue=epilogue),
        grid=(n // tn, r // tm),
        in_specs=in_specs,
        out_specs=pl.BlockSpec((tm, tn), lambda j, i: (i, j)),
        out_shape=jax.ShapeDtypeStruct((r, n), out_dtype),
        compiler_params=pltpu.CompilerParams(
            dimension_semantics=("arbitrary", "arbitrary"), vmem_limit_bytes=VMEM_LIMIT),
        name=name,
    )(*args)


def _mm_k2_kernel(x_ref, w_ref, gate_ref, add_ref, o_ref, acc_ref):
    kk = pl.program_id(2)
    part = jnp.dot(x_ref[...], w_ref[...], preferred_element_type=F32)

    @pl.when(kk == 0)
    def _():
        acc_ref[...] = part

    @pl.when(kk == 1)
    def _():
        y = acc_ref[...] + part
        o_ref[...] = (y * gate_ref[...].astype(F32) + add_ref[...].astype(F32)).astype(o_ref.dtype)


def _matmul_k2_gate_add(x, w, gates, gate_off, add, out_dtype, tm, tn, name):
    r, k = x.shape
    n = w.shape[1]
    kh = k // 2
    return pl.pallas_call(
        _mm_k2_kernel,
        grid=(n // tn, r // tm, 2),
        in_specs=[
            pl.BlockSpec((tm, kh), lambda j, i, kk: (i, kk)),
            pl.BlockSpec((kh, tn), lambda j, i, kk: (kk, j)),
            pl.BlockSpec((tm, tn), lambda j, i, kk: (i, j + gate_off)),
            pl.BlockSpec((tm, tn), lambda j, i, kk: (i, j)),
        ],
        out_specs=pl.BlockSpec((tm, tn), lambda j, i, kk: (i, j)),
        out_shape=jax.ShapeDtypeStruct((r, n), out_dtype),
        scratch_shapes=[pltpu.VMEM((tm, tn), F32)],
        compiler_params=pltpu.CompilerParams(
            dimension_semantics=("arbitrary", "arbitrary", "arbitrary"), vmem_limit_bytes=VMEM_LIMIT),
        name=name,
    )(x, w, gates, add)


def _dt_kernel(x_ref, w_ref, bias_ref, o_ref, *, tm, t_rows):
    i = pl.program_id(0)
    raw = jnp.dot(x_ref[...], w_ref[...], preferred_element_type=F32)
    dt = _softplus(raw + bias_ref[...])
    tiles_per_seq = t_rows // tm
    row_in_seq = lax.rem(i, tiles_per_seq) * tm + lax.broadcasted_iota(jnp.int32, dt.shape, 0)
    o_ref[...] = jnp.where(row_in_seq < PAD, 0.0, dt)


def _dt_proj(h2, w_dt, dt_bias, tm, t_rows):
    r, k = h2.shape
    nh = w_dt.shape[1]
    return pl.pallas_call(
        functools.partial(_dt_kernel, tm=tm, t_rows=t_rows),
        grid=(r // tm,),
        in_specs=[
            pl.BlockSpec((tm, k), lambda i: (i, 0)),
            pl.BlockSpec((k, nh), lambda i: (0, 0)),
            pl.BlockSpec((1, nh), lambda i: (0, 0)),
        ],
        out_specs=pl.BlockSpec((tm, nh), lambda i: (i, 0)),
        out_shape=jax.ShapeDtypeStruct((r, nh), F32),
        compiler_params=pltpu.CompilerParams(
            dimension_semantics=("arbitrary",), vmem_limit_bytes=VMEM_LIMIT),
        name="dt_proj",
    )(h2, w_dt, dt_bias.reshape(1, nh))


def _conv_branch_kernel(v_ref, sg_ref, w_ref, cb_ref, g_ref, b_ref, o_ref, buf_ref, cv_ref,
                        *, tl, col_w, row_b):
    t = pl.program_id(1)
    c_dim = v_ref.shape[-1]

    @pl.when(t == 0)
    def _():
        buf_ref[0:HALO, :] = jnp.zeros((HALO, c_dim), F32)

    @pl.when(t > 0)
    def _():
        buf_ref[0:HALO, :] = buf_ref[tl:tl + HALO, :]

    buf_ref[HALO:HALO + tl, :] = v_ref[0].astype(F32)

    base = HALO - (CONV_WIDTH - 1)
    def col_block(cb, carry):
        cols = pl.ds(pl.multiple_of(cb * col_w, col_w), col_w)
        for r0 in range(0, tl, row_b):
            acc = jnp.zeros((row_b, col_w), F32) + cb_ref[:, cols]
            for k in range(CONV_WIDTH):
                acc = acc + buf_ref[r0 + base + k:r0 + base + k + row_b, cols] * w_ref[k:k + 1, cols]
            cv_ref[r0:r0 + row_b, cols] = acc
        return carry
    lax.fori_loop(0, c_dim // col_w, col_block, 0)

    y = _layer_norm_rows(cv_ref[...], g_ref[...], b_ref[...])
    o_ref[0] = (_silu(y) * sg_ref[0].astype(F32)).astype(o_ref.dtype)


def _conv_branch(v3, sg3, conv_w, conv_b, ln_g, ln_b):
    nb, t_rows, c = v3.shape
    tl = CHUNK
    row = lambda a: a.reshape(1, c)
    return pl.pallas_call(
        functools.partial(_conv_branch_kernel, tl=tl, col_w=512, row_b=32),
        grid=(nb, t_rows // tl),
        in_specs=[
            pl.BlockSpec((1, tl, c), lambda bi, t: (bi, t, 0)),
            pl.BlockSpec((1, tl, c), lambda bi, t: (bi, t, 0)),
            pl.BlockSpec((CONV_WIDTH, c), lambda bi, t: (0, 0)),
            pl.BlockSpec((1, c), lambda bi, t: (0, 0)),
            pl.BlockSpec((1, c), lambda bi, t: (0, 0)),
            pl.BlockSpec((1, c), lambda bi, t: (0, 0)),
        ],
        out_specs=pl.BlockSpec((1, tl, c), lambda bi, t: (bi, t, 0)),
        out_shape=jax.ShapeDtypeStruct((nb, t_rows, c), BF16),
        scratch_shapes=[pltpu.VMEM((HALO + tl, c), F32), pltpu.VMEM((tl, c), F32)],
        compiler_params=pltpu.CompilerParams(
            dimension_semantics=("arbitrary", "arbitrary"), vmem_limit_bytes=VMEM_LIMIT),
        name="conv_branch",
    )(v3, sg3, conv_w, row(conv_b), row(ln_g), row(ln_b))


def _ssd_kernel(xg_ref, bg_ref, cg_ref, dt_ref, dtt_ref, sz_ref,
                wx_ref, wb_ref, wc_ref, bx_ref, bb_ref, bc_ref,
                alog_ref, alogt_ref, dskip_ref, ng_ref,
                o_ref, xbuf, bbuf, cbuf, state):
    c = pl.program_id(2)
    q = CHUNK
    hp = xg_ref.shape[-1]
    n_pairs = hp // (2 * SSD_HEAD_DIM)

    @pl.when(c == 0)
    def _():
        xbuf[0:SUBLANES, :] = jnp.zeros((SUBLANES, xbuf.shape[1]), F32)
        bbuf[0:SUBLANES, :] = jnp.zeros((SUBLANES, bbuf.shape[1]), F32)
        cbuf[0:SUBLANES, :] = jnp.zeros((SUBLANES, cbuf.shape[1]), F32)
        state[...] = jnp.zeros(state.shape, F32)

    def conv_silu(buf, new_ref, w_ref, bias_ref):
        buf[SUBLANES:SUBLANES + q, :] = new_ref[0].astype(F32)
        base = SUBLANES - (SSD_CONV_WIDTH - 1)
        acc = bias_ref[...] + buf[base:base + q, :] * w_ref[0:1, :]
        for k in range(1, SSD_CONV_WIDTH):
            acc = acc + buf[base + k:base + k + q, :] * w_ref[k:k + 1, :]
        buf[0:SUBLANES, :] = buf[q:q + SUBLANES, :]
        return _silu(acc)

    xs = conv_silu(xbuf, xg_ref, wx_ref, bx_ref)
    bm = conv_silu(bbuf, bg_ref, wb_ref, bb_ref)
    cm = conv_silu(cbuf, cg_ref, wc_ref, bc_ref)

    dt = dt_ref[0, 0]
    dtt = dtt_ref[0, 0]
    a_row = -jnp.exp(alog_ref[0])
    a_col = -jnp.exp(alogt_ref[0])

    rows = lax.broadcasted_iota(jnp.int32, (q, q), 0)
    cols = lax.broadcasted_iota(jnp.int32, (q, q), 1)
    causal = cols <= rows
    tri = jnp.where(causal, 1.0, 0.0).astype(F32)
    tri_t = jnp.where(rows <= cols, 1.0, 0.0).astype(F32)
    a_cs = jnp.dot(tri, dt * a_row, preferred_element_type=F32,
                   precision=lax.Precision.HIGHEST)
    a_cs_t = jnp.dot(dtt * a_col, tri_t, preferred_element_type=F32,
                     precision=lax.Precision.HIGHEST)

    bm_b = bm.astype(BF16)
    cm_b = cm.astype(BF16)
    cb = lax.dot_general(cm_b, bm_b, (((1,), (1,)), ((), ())), preferred_element_type=F32)
    bm_t = bm.T.astype(BF16)

    lane = lax.broadcasted_iota(jnp.int32, (q, 2 * SSD_HEAD_DIM), 1)
    first = lane < SSD_HEAD_DIM

    y_parts = []
    for j in range(n_pairs):
        h0, h1 = 2 * j, 2 * j + 1
        lo, hi = j * 2 * SSD_HEAD_DIM, (j + 1) * 2 * SSD_HEAD_DIM
        ac0 = jnp.broadcast_to(a_cs[:, h0:h0 + 1], (q, q))
        ac1 = jnp.broadcast_to(a_cs[:, h1:h1 + 1], (q, q))
        ac_pair = jnp.where(first, ac0, ac1)
        dt_pair = jnp.where(first, jnp.broadcast_to(dt[:, h0:h0 + 1], (q, q)),
                            jnp.broadcast_to(dt[:, h1:h1 + 1], (q, q)))
        tot_pair = ac_pair[q - 1:q, :]
        x_p = xs[:, lo:hi]
        xdt = x_p * dt_pair
        xdt_b = xdt.astype(BF16)

        seg0 = jnp.where(causal, ac0 - a_cs_t[h0:h0 + 1, :], -jnp.inf)
        seg1 = jnp.where(causal, ac1 - a_cs_t[h1:h1 + 1, :], -jnp.inf)
        m0 = (cb * jnp.exp(seg0)).astype(BF16)
        m1 = (cb * jnp.exp(seg1)).astype(BF16)
        lhs = jnp.concatenate([m0, m1], axis=1)
        zero = jnp.zeros_like(xdt_b)
        rhs = jnp.concatenate([jnp.where(first, xdt_b, zero), jnp.where(first, zero, xdt_b)], axis=0)
        y_diag = jnp.dot(lhs, rhs, preferred_element_type=F32)

        st = state[:, lo:hi]
        y_off = jnp.dot(cm_b, st.astype(BF16), preferred_element_type=F32) * jnp.exp(ac_pair)
        y_parts.append(y_diag + y_off + dskip_ref[:, lo:hi] * x_p)

        xw = (xdt * jnp.exp(tot_pair - ac_pair)).astype(BF16)
        state[:, lo:hi] = st * jnp.exp(tot_pair) + jnp.dot(bm_t, xw, preferred_element_type=F32)

    y = jnp.concatenate(y_parts, axis=1)
    yz = y * sz_ref[0].astype(F32)
    ms = jnp.mean(yz * yz, axis=-1, keepdims=True)
    o_ref[0] = (yz * lax.rsqrt(ms + RMS_EPS) * ng_ref[...]).astype(o_ref.dtype)


def _ssd_branch(xbc3, dt2, sz3, ssd_conv_w, ssd_conv_b, a_log, d_skip, norm_g):
    nb, t_rows, _ = xbc3.shape
    inner = sz3.shape[-1]
    g = SSD_GROUPS
    gw = inner // g
    n = SSD_STATE
    heads = a_log.shape[0]
    hg = heads // g
    nc = t_rows // CHUNK
    xblk = inner // gw
    b_blk0 = inner // n
    c_blk0 = b_blk0 + g

    dt4 = dt2.reshape(nb, t_rows, g, hg)
    dt_g = dt4.transpose(0, 2, 1, 3)
    dt_gt = dt4.transpose(0, 2, 3, 1)
    bias = ssd_conv_b.reshape(1, -1)
    dskip_x = jnp.repeat(d_skip, SSD_HEAD_DIM).reshape(1, inner)
    del xblk

    return pl.pallas_call(
        _ssd_kernel,
        grid=(nb, g, nc),
        in_specs=[
            pl.BlockSpec((1, CHUNK, gw), lambda bi, gi, ci: (bi, ci, gi)),
            pl.BlockSpec((1, CHUNK, n), lambda bi, gi, ci: (bi, ci, b_blk0 + gi)),
            pl.BlockSpec((1, CHUNK, n), lambda bi, gi, ci: (bi, ci, c_blk0 + gi)),
            pl.BlockSpec((1, 1, CHUNK, hg), lambda bi, gi, ci: (bi, gi, ci, 0)),
            pl.BlockSpec((1, 1, hg, CHUNK), lambda bi, gi, ci: (bi, gi, 0, ci)),
            pl.BlockSpec((1, CHUNK, gw), lambda bi, gi, ci: (bi, ci, gi)),
            pl.BlockSpec((SSD_CONV_WIDTH, gw), lambda bi, gi, ci: (0, gi)),
            pl.BlockSpec((SSD_CONV_WIDTH, n), lambda bi, gi, ci: (0, b_blk0 + gi)),
            pl.BlockSpec((SSD_CONV_WIDTH, n), lambda bi, gi, ci: (0, c_blk0 + gi)),
            pl.BlockSpec((1, gw), lambda bi, gi, ci: (0, gi)),
            pl.BlockSpec((1, n), lambda bi, gi, ci: (0, b_blk0 + gi)),
            pl.BlockSpec((1, n), lambda bi, gi, ci: (0, c_blk0 + gi)),
            pl.BlockSpec((1, 1, hg), lambda bi, gi, ci: (gi, 0, 0)),
            pl.BlockSpec((1, hg, 1), lambda bi, gi, ci: (gi, 0, 0)),
            pl.BlockSpec((1, gw), lambda bi, gi, ci: (0, gi)),
            pl.BlockSpec((1, gw), lambda bi, gi, ci: (0, gi)),
        ],
        out_specs=pl.BlockSpec((1, CHUNK, gw), lambda bi, gi, ci: (bi, ci, gi)),
        out_shape=jax.ShapeDtypeStruct((nb, t_rows, inner), BF16),
        scratch_shapes=[
            pltpu.VMEM((SUBLANES + CHUNK, gw), F32),
            pltpu.VMEM((SUBLANES + CHUNK, n), F32),
            pltpu.VMEM((SUBLANES + CHUNK, n), F32),
            pltpu.VMEM((n, gw), F32),
        ],
        compiler_params=pltpu.CompilerParams(
            dimension_semantics=("arbitrary", "arbitrary", "arbitrary"), vmem_limit_bytes=VMEM_LIMIT),
        name="ssd_branch",
    )(xbc3, xbc3, xbc3, dt_g, dt_gt, sz3,
      ssd_conv_w, ssd_conv_w, ssd_conv_w, bias, bias, bias,
      a_log.reshape(g, 1, hg), a_log.reshape(g, hg, 1), dskip_x, norm_g.reshape(1, inner))


def _ln_out_kernel(x_ref, sub_ref, gi_ref, bi_ref, go_ref, bo_ref, o_ref):
    h = _layer_norm_rows(x_ref[0], gi_ref[...], bi_ref[...])
    o_ref[0] = _layer_norm_rows(DEEPNORM_ALPHA * h + sub_ref[0].astype(F32), go_ref[...], bo_ref[...])


def _ln_out(x, sub3, gi, bi, go, bo):
    nb, seq, d = x.shape
    row = lambda a: a.reshape(1, d)
    vec = pl.BlockSpec((1, d), lambda b_, t: (0, 0))
    return pl.pallas_call(
        _ln_out_kernel,
        grid=(nb, seq // CHUNK),
        in_specs=[
            pl.BlockSpec((1, CHUNK, d), lambda b_, t: (b_, t, 0)),
            pl.BlockSpec((1, CHUNK, d), lambda b_, t: (b_, t + 1, 0)),
            vec, vec, vec, vec,
        ],
        out_specs=pl.BlockSpec((1, CHUNK, d), lambda b_, t: (b_, t, 0)),
        out_shape=jax.ShapeDtypeStruct((nb, seq, d), F32),
        name="ln_out",
    )(x, sub3, row(gi), row(bi), row(go), row(bo))


def kernel(x, meta_tokens, ln_in_g, ln_in_b, w_in, b_gate, conv_w, conv_b, conv_ln_g, conv_ln_b,
           w_conv_out, ssd_conv_w, ssd_conv_b, dt_bias, a_log, d_skip, ssd_norm_g, w_ssd_out,
           w_out, ln_out_g, ln_out_b):
    nb, seq, d = x.shape
    assert w_in.shape[0] == 1, "single-layer trunk"
    assert seq % CHUNK == 0
    t_rows = seq + CHUNK
    r = nb * t_rows
    inner = w_ssd_out.shape[1]
    heads = a_log.shape[1]
    xbc_w = ssd_conv_w.shape[2]
    tm = t_rows // 4
    tn = 1024
    assert t_rows % 4 == 0 and tm % 16 == 0

    o_val, o_glu, o_gate, o_z = 0, d, 2 * d, 3 * d
    o_xbc = o_z + inner
    o_dt = o_xbc + xbc_w
    o_g = o_dt + heads
    w = w_in[0]
    cut = lambda a, b_: w[:, a:b_].astype(BF16)

    h3 = _ln_in(x, meta_tokens, ln_in_g, ln_in_b)
    h2 = h3.reshape(r, d)

    v = _matmul(h2, [cut(o_val, o_glu), cut(o_glu, o_gate)], [],
                lambda a, e: a[0] * _sigmoid(a[1]), BF16, tm, tn // 2, "in_proj_glu")
    sg = _matmul(h2, [cut(o_gate, o_z)], [], lambda a, e: _silu(a[0]), BF16, tm, tn, "in_proj_gate")
    sz = _matmul(h2, [cut(o_z, o_xbc)], [], lambda a, e: _silu(a[0]), BF16, tm, tn, "in_proj_z")
    xbc = _matmul(h2, [cut(o_xbc, o_dt)], [], lambda a, e: a[0], BF16, tm, tn, "in_proj_xbc")
    dt = _dt_proj(h2, cut(o_dt, o_g), dt_bias[0], tm, t_rows)
    gates = _matmul(h2, [cut(o_g, o_g + 2 * d)], [(b_gate[0].reshape(1, 2 * d), 0)],
                    lambda a, e: _sigmoid(a[0] + e[0]), BF16, tm, tn, "in_proj_merge_gates")

    u = _conv_branch(v.reshape(nb, t_rows, d), sg.reshape(nb, t_rows, d),
                     conv_w[0], conv_b[0], conv_ln_g[0], conv_ln_b[0])
    yn = _ssd_branch(xbc.reshape(nb, t_rows, xbc_w), dt, sz.reshape(nb, t_rows, inner),
                     ssd_conv_w[0], ssd_conv_b[0], a_log[0], d_skip[0], ssd_norm_g[0])

    gated_conv = _matmul(u.reshape(r, d), [w_conv_out[0].astype(BF16)], [(gates, 0)],
                         lambda a, e: a[0] * e[0].astype(F32), BF16, tm, tn, "conv_out_proj")
    merged = _matmul_k2_gate_add(yn.reshape(r, inner), w_ssd_out[0].astype(BF16), gates, 2 * d // tn,
                                 gated_conv, BF16, tm, tn // 2, "ssd_out_proj_merge")
    sub = _matmul(merged, [w_out[0].astype(BF16)], [], lambda a, e: a[0], F32, tm, tn, "out_proj")

    return _ln_out(x, sub.reshape(nb, t_rows, d), ln_in_g, ln_in_b, ln_out_g[0], ln_out_b[0])
```

```python
import functools

import jax
import jax.numpy as jnp
from jax import lax
from jax.experimental import pallas as pl
from jax.experimental.pallas import tpu as pltpu

F32 = jnp.float32
BF16 = jnp.bfloat16

N_META = 16
CHUNK = 128
PAD = CHUNK - N_META
CONV_WIDTH = 31
SSD_HEAD_DIM = 64
SSD_GROUPS = 8
SSD_STATE = 128
SSD_CONV_WIDTH = 4
LN_EPS = 1e-5
RMS_EPS = 1e-5
DEEPNORM_ALPHA = 2.0 ** 0.25

SUBLANES = 8
LANES = 128
MXU_COLS = 256
VMEM_LIMIT = 56 * 1024 * 1024


def _sigmoid(x):
    return 1.0 / (1.0 + jnp.exp(-x))


def _silu(x):
    return x * _sigmoid(x)


def _softplus(x):
    return jnp.maximum(x, 0.0) + jnp.log1p(jnp.exp(-jnp.abs(x)))


def _layer_norm_rows(v, g, b):
    mu = jnp.mean(v, axis=-1, keepdims=True)
    d = v - mu
    var = jnp.mean(d * d, axis=-1, keepdims=True)
    return d * lax.rsqrt(var + LN_EPS) * g + b


def _ln_in_kernel(x_ref, meta_ref, g_ref, b_ref, o_ref):
    t = pl.program_id(1)
    g = g_ref[...]
    b = b_ref[...]

    @pl.when(t == 0)
    def _():
        d = o_ref.shape[-1]
        o_ref[0, :PAD, :] = jnp.zeros((PAD, d), o_ref.dtype)
        o_ref[0, PAD:, :] = _layer_norm_rows(meta_ref[...], g, b).astype(o_ref.dtype)

    @pl.when(t > 0)
    def _():
        o_ref[0] = _layer_norm_rows(x_ref[0], g, b).astype(o_ref.dtype)


def _ln_in(x, meta, g, b):
    nb, seq, d = x.shape
    nt = seq // CHUNK + 1
    return pl.pallas_call(
        _ln_in_kernel,
        grid=(nb, nt),
        in_specs=[
            pl.BlockSpec((1, CHUNK, d), lambda bi, t: (bi, jnp.maximum(t - 1, 0), 0)),
            pl.BlockSpec((N_META, d), lambda bi, t: (0, 0)),
            pl.BlockSpec((1, d), lambda bi, t: (0, 0)),
            pl.BlockSpec((1, d), lambda bi, t: (0, 0)),
        ],
        out_specs=pl.BlockSpec((1, CHUNK, d), lambda bi, t: (bi, t, 0)),
        out_shape=jax.ShapeDtypeStruct((nb, nt * CHUNK, d), BF16),
        name="ln_in",
    )(x, meta, g.reshape(1, d), b.reshape(1, d))


def _mm_kernel(*refs, n_w, epilogue):
    x_ref = refs[0]
    w_refs = refs[1:1 + n_w]
    e_refs = refs[1 + n_w:-1]
    o_ref = refs[-1]
    x = x_ref[...]
    accs = [jnp.dot(x, w[...], preferred_element_type=F32) for w in w_refs]
    o_ref[...] = epilogue(accs, [e[...] for e in e_refs]).astype(o_ref.dtype)


def _matmul(x, ws, extras, epilogue, out_dtype, tm, tn, name):
    r, k = x.shape
    n = ws[0].shape[1]
    in_specs = [pl.BlockSpec((tm, k), lambda j, i: (i, 0))]
    in_specs += [pl.BlockSpec((k, tn), lambda j, i: (0, j)) for _ in ws]
    args = [x] + list(ws)
    for arr, off in extras:
        if arr.shape[0] == 1:
            in_specs.append(pl.BlockSpec((1, tn), lambda j, i, off=off: (0, j + off)))
        else:
            in_specs.append(pl.BlockSpec((tm, tn), lambda j, i, off=off: (i, j + off)))
        args.append(arr)
    return pl.pallas_call(
        functools.partial(_mm_kernel, n_w=len(ws), epilogue=epilogue),
        grid=(n // tn, r // tm),
        in_specs=in_specs,
        out_specs=pl.BlockSpec((tm, tn), lambda j, i: (i, j)),
        out_shape=jax.ShapeDtypeStruct((r, n), out_dtype),
        compiler_params=pltpu.CompilerParams(
            dimension_semantics=("arbitrary", "arbitrary"), vmem_limit_bytes=VMEM_LIMIT),
        name=name,
    )(*args)


def _mm_k2_kernel(x_ref, w_ref, gate_ref, add_ref, o_ref, acc_ref):
    kk = pl.program_id(2)
    part = jnp.dot(x_ref[...], w_ref[...], preferred_element_type=F32)

    @pl.when(kk == 0)
    def _():
        acc_ref[...] = part

    @pl.when(kk == 1)
    def _():
        y = acc_ref[...] + part
        o_ref[...] = (y * gate_ref[...].astype(F32) + add_ref[...].astype(F32)).astype(o_ref.dtype)


def _matmul_k2_gate_add(x, w, gates, gate_off, add, out_dtype, tm, tn, name):
    r, k = x.shape
    n = w.shape[1]
    kh = k // 2
    return pl.pallas_call(
        _mm_k2_kernel,
        grid=(n // tn, r // tm, 2),
        in_specs=[
            pl.BlockSpec((tm, kh), lambda j, i, kk: (i, kk)),
            pl.BlockSpec((kh, tn), lambda j, i, kk: (kk, j)),
            pl.BlockSpec((tm, tn), lambda j, i, kk: (i, j + gate_off)),
            pl.BlockSpec((tm, tn), lambda j, i, kk: (i, j)),
        ],
        out_specs=pl.BlockSpec((tm, tn), lambda j, i, kk: (i, j)),
        out_shape=jax.ShapeDtypeStruct((r, n), out_dtype),
        scratch_shapes=[pltpu.VMEM((tm, tn), F32)],
        compiler_params=pltpu.CompilerParams(
            dimension_semantics=("arbitrary", "arbitrary", "arbitrary"), vmem_limit_bytes=VMEM_LIMIT),
        name=name,
    )(x, w, gates, add)


def _mm_conv_kernel(*refs, n_w, taps, halo, tm, tn, n_row_tiles, n_tiles, tiles_per_seq, act, row_group):
    x_ref = refs[0]
    w_refs = refs[1:1 + n_w]
    cw_ref, cb_ref, o_ref, vbuf, stage = refs[1 + n_w:]
    n = pl.program_id(0)
    stride = tm // SUBLANES

    @pl.when(n == 0)
    def _():
        vbuf[...] = jnp.zeros(vbuf.shape, F32)

    cur = jnp.minimum(n, n_tiles - 1)
    starts_seq = lax.rem(lax.rem(cur, n_row_tiles), tiles_per_seq) == 0
    slabs_per_group = MXU_COLS // LANES

    def column_group(g, carry):
        col0 = pl.multiple_of(g * MXU_COLS, MXU_COLS)
        for s2 in range(slabs_per_group):
            s = g * slabs_per_group + s2
            lanes = pl.ds(pl.multiple_of(col0 + s2 * LANES, LANES), LANES)
            for i0 in range(0, stride, row_group):
                accs = [cb_ref[:, lanes]] * row_group
                for k in range(taps):
                    wk = cw_ref[k * SUBLANES:(k + 1) * SUBLANES, lanes]
                    for ii in range(row_group):
                        start = halo + i0 + ii - (taps - 1) + k
                        accs[ii] = accs[ii] + vbuf[s, pl.ds(start, SUBLANES, stride=stride), :] * wk
                for ii in range(row_group):
                    stage[s, pl.ds(i0 + ii, SUBLANES, stride=stride), :] = act(accs[ii])
            o_ref[:, lanes] = stage[s].astype(o_ref.dtype)

        x = x_ref[...]
        cols = pl.ds(col0, MXU_COLS)
        accs = [jnp.dot(x, w[:, cols], preferred_element_type=F32) for w in w_refs]
        v = accs[0] * _sigmoid(accs[1]) if n_w == 2 else accs[0]
        for s2 in range(slabs_per_group):
            s = g * slabs_per_group + s2
            vbuf[s, 0:halo, :] = jnp.where(starts_seq, 0.0, vbuf[s, tm:tm + halo, :])
            vbuf[s, halo:halo + tm, :] = v[:, s2 * LANES:(s2 + 1) * LANES]
        return carry

    lax.fori_loop(0, tn // MXU_COLS, column_group, 0)


def _matmul_conv(x, ws, conv_w, conv_b, act, tm, tn, tiles_per_seq, name):
    r, k = x.shape
    n_cols = ws[0].shape[1]
    taps = conv_w.shape[0]
    halo = -(-(taps - 1) // SUBLANES) * SUBLANES
    n_row_tiles = r // tm
    n_tiles = n_row_tiles * (n_cols // tn)
    assert tm % (2 * SUBLANES) == 0 and tn % MXU_COLS == 0
    row_group = 12
    assert (tm // SUBLANES) % row_group == 0
    rep = lambda a: jnp.broadcast_to(a[:, None, :], (a.shape[0], SUBLANES, n_cols)).reshape(-1, n_cols)

    def cur_tile(n):
        c = jnp.minimum(n, n_tiles - 1)
        return c % n_row_tiles, c // n_row_tiles

    def lag_tile(n):
        c = jnp.maximum(n - 1, 0)
        return c % n_row_tiles, c // n_row_tiles

    in_specs = [pl.BlockSpec((tm, k), lambda n: (cur_tile(n)[0], 0))]
    in_specs += [pl.BlockSpec((k, tn), lambda n: (0, cur_tile(n)[1])) for _ in ws]
    in_specs += [pl.BlockSpec((taps * SUBLANES, tn), lambda n: (0, lag_tile(n)[1])),
                 pl.BlockSpec((SUBLANES, tn), lambda n: (0, lag_tile(n)[1]))]
    return pl.pallas_call(
        functools.partial(_mm_conv_kernel, n_w=len(ws), taps=taps, halo=halo, tm=tm, tn=tn,
                          n_row_tiles=n_row_tiles, n_tiles=n_tiles, tiles_per_seq=tiles_per_seq, act=act,
                          row_group=row_group),
        grid=(n_tiles + 1,),
        in_specs=in_specs,
        out_specs=pl.BlockSpec((tm, tn), lambda n: lag_tile(n)),
        out_shape=jax.ShapeDtypeStruct((r, n_cols), BF16),
        scratch_shapes=[pltpu.VMEM((tn // LANES, halo + tm, LANES), F32),
                        pltpu.VMEM((tn // LANES, tm, LANES), F32)],
        compiler_params=pltpu.CompilerParams(
            dimension_semantics=("arbitrary",), vmem_limit_bytes=VMEM_LIMIT),
        name=name,
    )(x, *ws, rep(conv_w), rep(conv_b.reshape(1, n_cols)))


def _dt_kernel(x_ref, w_ref, bias_ref, o_ref, *, tm, t_rows):
    i = pl.program_id(0)
    raw = jnp.dot(x_ref[...], w_ref[...], preferred_element_type=F32)
    dt = _softplus(raw + bias_ref[...])
    tiles_per_seq = t_rows // tm
    row_in_seq = lax.rem(i, tiles_per_seq) * tm + lax.broadcasted_iota(jnp.int32, dt.shape, 0)
    o_ref[...] = jnp.where(row_in_seq < PAD, 0.0, dt)


def _dt_proj(h2, w_dt, dt_bias, tm, t_rows):
    r, k = h2.shape
    nh = w_dt.shape[1]
    return pl.pallas_call(
        functools.partial(_dt_kernel, tm=tm, t_rows=t_rows),
        grid=(r // tm,),
        in_specs=[
            pl.BlockSpec((tm, k), lambda i: (i, 0)),
            pl.BlockSpec((k, nh), lambda i: (0, 0)),
            pl.BlockSpec((1, nh), lambda i: (0, 0)),
        ],
        out_specs=pl.BlockSpec((tm, nh), lambda i: (i, 0)),
        out_shape=jax.ShapeDtypeStruct((r, nh), F32),
        compiler_params=pltpu.CompilerParams(
            dimension_semantics=("arbitrary",), vmem_limit_bytes=VMEM_LIMIT),
        name="dt_proj",
    )(h2, w_dt, dt_bias.reshape(1, nh))


def _ln_act_kernel(cv_ref, sg_ref, g_ref, b_ref, o_ref):
    y = _layer_norm_rows(cv_ref[...].astype(F32), g_ref[...], b_ref[...])
    o_ref[...] = (_silu(y) * sg_ref[...].astype(F32)).astype(o_ref.dtype)


def _ln_act(cv, sg, ln_g, ln_b, tl):
    r, c = cv.shape
    vec = pl.BlockSpec((1, c), lambda i: (0, 0))
    tile = pl.BlockSpec((tl, c), lambda i: (i, 0))
    return pl.pallas_call(
        _ln_act_kernel,
        grid=(r // tl,),
        in_specs=[tile, tile, vec, vec],
        out_specs=tile,
        out_shape=jax.ShapeDtypeStruct((r, c), BF16),
        compiler_params=pltpu.CompilerParams(
            dimension_semantics=("arbitrary",), vmem_limit_bytes=VMEM_LIMIT),
        name="conv_ln_act",
    )(cv, sg, ln_g.reshape(1, c), ln_b.reshape(1, c))


def _ssd_kernel(xg_ref, bg_ref, cg_ref, dt_ref, dtt_ref, sz_ref,
                alog_ref, alogt_ref, dskip_ref, ng_ref, o_ref, state):
    c = pl.program_id(2)
    q = CHUNK
    hp = xg_ref.shape[-1]
    n_pairs = hp // (2 * SSD_HEAD_DIM)

    @pl.when(c == 0)
    def _():
        state[...] = jnp.zeros(state.shape, F32)

    xs = xg_ref[0].astype(F32)
    bm_b = bg_ref[0]
    cm_b = cg_ref[0]

    dt = dt_ref[0, 0]
    dtt = dtt_ref[0, 0]
    a_row = -jnp.exp(alog_ref[0])
    a_col = -jnp.exp(alogt_ref[0])

    rows = lax.broadcasted_iota(jnp.int32, (q, q), 0)
    cols = lax.broadcasted_iota(jnp.int32, (q, q), 1)
    causal = cols <= rows
    tri = jnp.where(causal, 1.0, 0.0).astype(F32)
    tri_t = jnp.where(rows <= cols, 1.0, 0.0).astype(F32)
    a_cs = jnp.dot(tri, dt * a_row, preferred_element_type=F32,
                   precision=lax.Precision.HIGHEST)
    a_cs_t = jnp.dot(dtt * a_col, tri_t, preferred_element_type=F32,
                     precision=lax.Precision.HIGHEST)

    cb = lax.dot_general(cm_b, bm_b, (((1,), (1,)), ((), ())), preferred_element_type=F32)
    bm_t = bm_b.astype(F32).T.astype(BF16)

    lane = lax.broadcasted_iota(jnp.int32, (q, 2 * SSD_HEAD_DIM), 1)
    first = lane < SSD_HEAD_DIM

    y_parts = []
    for j in range(n_pairs):
        h0, h1 = 2 * j, 2 * j + 1
        lo, hi = j * 2 * SSD_HEAD_DIM, (j + 1) * 2 * SSD_HEAD_DIM
        ac0 = jnp.broadcast_to(a_cs[:, h0:h0 + 1], (q, q))
        ac1 = jnp.broadcast_to(a_cs[:, h1:h1 + 1], (q, q))
        ac_pair = jnp.where(first, ac0, ac1)
        dt_pair = jnp.where(first, jnp.broadcast_to(dt[:, h0:h0 + 1], (q, q)),
                            jnp.broadcast_to(dt[:, h1:h1 + 1], (q, q)))
        tot_pair = ac_pair[q - 1:q, :]
        x_p = xs[:, lo:hi]
        xdt = x_p * dt_pair
        xdt_b = xdt.astype(BF16)

        seg0 = jnp.where(causal, ac0 - a_cs_t[h0:h0 + 1, :], -jnp.inf)
        seg1 = jnp.where(causal, ac1 - a_cs_t[h1:h1 + 1, :], -jnp.inf)
        m0 = (cb * jnp.exp(seg0)).astype(BF16)
        m1 = (cb * jnp.exp(seg1)).astype(BF16)
        lhs = jnp.concatenate([m0, m1], axis=1)
        zero = jnp.zeros_like(xdt_b)
        rhs = jnp.concatenate([jnp.where(first, xdt_b, zero), jnp.where(first, zero, xdt_b)], axis=0)
        y_diag = jnp.dot(lhs, rhs, preferred_element_type=F32)

        st = state[:, lo:hi]
        y_off = jnp.dot(cm_b, st.astype(BF16), preferred_element_type=F32) * jnp.exp(ac_pair)
        y_parts.append(y_diag + y_off + dskip_ref[:, lo:hi] * x_p)

        xw = (xdt * jnp.exp(tot_pair - ac_pair)).astype(BF16)
        state[:, lo:hi] = st * jnp.exp(tot_pair) + jnp.dot(bm_t, xw, preferred_element_type=F32)

    y = jnp.concatenate(y_parts, axis=1)
    yz = y * sz_ref[0].astype(F32)
    ms = jnp.mean(yz * yz, axis=-1, keepdims=True)
    o_ref[0] = (yz * lax.rsqrt(ms + RMS_EPS) * ng_ref[...]).astype(o_ref.dtype)


def _ssd_branch(xbc3, dt2, sz3, a_log, d_skip, norm_g):
    nb, t_rows, _ = xbc3.shape
    inner = sz3.shape[-1]
    g = SSD_GROUPS
    gw = inner // g
    n = SSD_STATE
    heads = a_log.shape[0]
    hg = heads // g
    nc = t_rows // CHUNK
    b_blk0 = inner // n
    c_blk0 = b_blk0 + g

    dt4 = dt2.reshape(nb, t_rows, g, hg)
    dt_g = dt4.transpose(0, 2, 1, 3)
    dt_gt = dt4.transpose(0, 2, 3, 1)
    dskip_x = jnp.repeat(d_skip, SSD_HEAD_DIM).reshape(1, inner)

    return pl.pallas_call(
        _ssd_kernel,
        grid=(nb, g, nc),
        in_specs=[
            pl.BlockSpec((1, CHUNK, gw), lambda bi, gi, ci: (bi, ci, gi)),
            pl.BlockSpec((1, CHUNK, n), lambda bi, gi, ci: (bi, ci, b_blk0 + gi)),
            pl.BlockSpec((1, CHUNK, n), lambda bi, gi, ci: (bi, ci, c_blk0 + gi)),
            pl.BlockSpec((1, 1, CHUNK, hg), lambda bi, gi, ci: (bi, gi, ci, 0)),
            pl.BlockSpec((1, 1, hg, CHUNK), lambda bi, gi, ci: (bi, gi, 0, ci)),
            pl.BlockSpec((1, CHUNK, gw), lambda bi, gi, ci: (bi, ci, gi)),
            pl.BlockSpec((1, 1, hg), lambda bi, gi, ci: (gi, 0, 0)),
            pl.BlockSpec((1, hg, 1), lambda bi, gi, ci: (gi, 0, 0)),
            pl.BlockSpec((1, gw), lambda bi, gi, ci: (0, gi)),
            pl.BlockSpec((1, gw), lambda bi, gi, ci: (0, gi)),
        ],
        out_specs=pl.BlockSpec((1, CHUNK, gw), lambda bi, gi, ci: (bi, ci, gi)),
        out_shape=jax.ShapeDtypeStruct((nb, t_rows, inner), BF16),
        scratch_shapes=[pltpu.VMEM((n, gw), F32)],
        compiler_params=pltpu.CompilerParams(
            dimension_semantics=("arbitrary", "arbitrary", "arbitrary"), vmem_limit_bytes=VMEM_LIMIT),
        name="ssd_branch",
    )(xbc3, xbc3, xbc3, dt_g, dt_gt, sz3,
      a_log.reshape(g, 1, hg), a_log.reshape(g, hg, 1), dskip_x, norm_g.reshape(1, inner))


def _ln_out_kernel(x_ref, sub_ref, gi_ref, bi_ref, go_ref, bo_ref, o_ref):
    h = _layer_norm_rows(x_ref[0], gi_ref[...], bi_ref[...])
    o_ref[0] = _layer_norm_rows(DEEPNORM_ALPHA * h + sub_ref[0].astype(F32), go_ref[...], bo_ref[...])


def _ln_out(x, sub3, gi, bi, go, bo):
    nb, seq, d = x.shape
    row = lambda a: a.reshape(1, d)
    vec = pl.BlockSpec((1, d), lambda b_, t: (0, 0))
    return pl.pallas_call(
        _ln_out_kernel,
        grid=(nb, seq // CHUNK),
        in_specs=[
            pl.BlockSpec((1, CHUNK, d), lambda b_, t: (b_, t, 0)),
            pl.BlockSpec((1, CHUNK, d), lambda b_, t: (b_, t + 1, 0)),
            vec, vec, vec, vec,
        ],
        out_specs=pl.BlockSpec((1, CHUNK, d), lambda b_, t: (b_, t, 0)),
        out_shape=jax.ShapeDtypeStruct((nb, seq, d), F32),
        name="ln_out",
    )(x, sub3, row(gi), row(bi), row(go), row(bo))


def kernel(x, meta_tokens, ln_in_g, ln_in_b, w_in, b_gate, conv_w, conv_b, conv_ln_g, conv_ln_b,
           w_conv_out, ssd_conv_w, ssd_conv_b, dt_bias, a_log, d_skip, ssd_norm_g, w_ssd_out,
           w_out, ln_out_g, ln_out_b):
    nb, seq, d = x.shape
    assert w_in.shape[0] == 1, "single-layer trunk"
    assert seq % CHUNK == 0
    t_rows = seq + CHUNK
    r = nb * t_rows
    inner = w_ssd_out.shape[1]
    heads = a_log.shape[1]
    xbc_w = ssd_conv_w.shape[2]
    tiles_per_seq = 4
    tm = t_rows // tiles_per_seq
    tn = 1024
    assert t_rows % tiles_per_seq == 0 and tm % 16 == 0

    o_val, o_glu, o_gate, o_z = 0, d, 2 * d, 3 * d
    o_xbc = o_z + inner
    o_dt = o_xbc + xbc_w
    o_g = o_dt + heads
    w = w_in[0]
    cut = lambda a, b_: w[:, a:b_].astype(BF16)

    h3 = _ln_in(x, meta_tokens, ln_in_g, ln_in_b)
    h2 = h3.reshape(r, d)

    cv = _matmul_conv(h2, [cut(o_val, o_glu), cut(o_glu, o_gate)], conv_w[0], conv_b[0],
                      lambda a: a, tm, tn // 2, tiles_per_seq, "in_proj_glu_conv")
    sg = _matmul(h2, [cut(o_gate, o_z)], [], lambda a, e: _silu(a[0]), BF16, tm, tn, "in_proj_gate")
    sz = _matmul(h2, [cut(o_z, o_xbc)], [], lambda a, e: _silu(a[0]), BF16, tm, tn, "in_proj_z")
    xbc = _matmul_conv(h2, [cut(o_xbc, o_dt)], ssd_conv_w[0], ssd_conv_b[0],
                       _silu, tm, tn, tiles_per_seq, "in_proj_xbc_conv")
    dt = _dt_proj(h2, cut(o_dt, o_g), dt_bias[0], tm, t_rows)
    gates = _matmul(h2, [cut(o_g, o_g + 2 * d)], [(b_gate[0].reshape(1, 2 * d), 0)],
                    lambda a, e: _sigmoid(a[0] + e[0]), BF16, tm, tn, "in_proj_merge_gates")

    u = _ln_act(cv, sg, conv_ln_g[0], conv_ln_b[0], tm // 6)
    yn = _ssd_branch(xbc.reshape(nb, t_rows, xbc_w), dt, sz.reshape(nb, t_rows, inner),
                     a_log[0], d_skip[0], ssd_norm_g[0])

    gated_conv = _matmul(u, [w_conv_out[0].astype(BF16)], [(gates, 0)],
                         lambda a, e: a[0] * e[0].astype(F32), BF16, tm, tn, "conv_out_proj")
    merged = _matmul_k2_gate_add(yn.reshape(r, inner), w_ssd_out[0].astype(BF16), gates, 2 * d // tn,
                                 gated_conv, BF16, tm, tn // 2, "ssd_out_proj_merge")
    sub = _matmul(merged, [w_out[0].astype(BF16)], [], lambda a, e: a[0], F32, tm, tn, "out_proj")

    return _ln_out(x, sub.reshape(nb, t_rows, d), ln_in_g, ln_in_b, ln_out_g[0], ln_out_b[0])
```

```python
import functools

import jax
import jax.numpy as jnp
from jax import lax
from jax.experimental import pallas as pl
from jax.experimental.pallas import tpu as pltpu

F32 = jnp.float32
BF16 = jnp.bfloat16

N_META = 16
CHUNK = 128
PAD = CHUNK - N_META
CONV_WIDTH = 31
SSD_HEAD_DIM = 64
SSD_GROUPS = 8
SSD_STATE = 128
SSD_CONV_WIDTH = 4
LN_EPS = 1e-5
RMS_EPS = 1e-5
DEEPNORM_ALPHA = 2.0 ** 0.25

SUBLANES = 8
LANES = 128
MXU_COLS = 256
VMEM_LIMIT = 56 * 1024 * 1024


def _sigmoid(x):
    return 1.0 / (1.0 + jnp.exp(-x))


def _silu(x):
    return x * _sigmoid(x)


def _softplus(x):
    return jnp.maximum(x, 0.0) + jnp.log1p(jnp.exp(-jnp.abs(x)))


def _layer_norm_rows(v, g, b):
    mu = jnp.mean(v, axis=-1, keepdims=True)
    d = v - mu
    var = jnp.mean(d * d, axis=-1, keepdims=True)
    return d * lax.rsqrt(var + LN_EPS) * g + b


def _ln_in_kernel(x_ref, meta_ref, g_ref, b_ref, o_ref):
    t = pl.program_id(1)
    g = g_ref[...]
    b = b_ref[...]

    @pl.when(t == 0)
    def _():
        d = o_ref.shape[-1]
        o_ref[0, :PAD, :] = jnp.zeros((PAD, d), o_ref.dtype)
        o_ref[0, PAD:, :] = _layer_norm_rows(meta_ref[...], g, b).astype(o_ref.dtype)

    @pl.when(t > 0)
    def _():
        o_ref[0] = _layer_norm_rows(x_ref[0], g, b).astype(o_ref.dtype)


def _ln_in(x, meta, g, b):
    nb, seq, d = x.shape
    nt = seq // CHUNK + 1
    return pl.pallas_call(
        _ln_in_kernel,
        grid=(nb, nt),
        in_specs=[
            pl.BlockSpec((1, CHUNK, d), lambda bi, t: (bi, jnp.maximum(t - 1, 0), 0)),
            pl.BlockSpec((N_META, d), lambda bi, t: (0, 0)),
            pl.BlockSpec((1, d), lambda bi, t: (0, 0)),
            pl.BlockSpec((1, d), lambda bi, t: (0, 0)),
        ],
        out_specs=pl.BlockSpec((1, CHUNK, d), lambda bi, t: (bi, t, 0)),
        out_shape=jax.ShapeDtypeStruct((nb, nt * CHUNK, d), BF16),
        name="ln_in",
    )(x, meta, g.reshape(1, d), b.reshape(1, d))


def _mm_kernel(*refs, n_w, epilogue):
    x_ref = refs[0]
    w_refs = refs[1:1 + n_w]
    e_refs = refs[1 + n_w:-1]
    o_ref = refs[-1]
    x = x_ref[...]
    accs = [jnp.dot(x, w[...], preferred_element_type=F32) for w in w_refs]
    o_ref[...] = epilogue(accs, [e[...] for e in e_refs]).astype(o_ref.dtype)


def _matmul(x, ws, extras, epilogue, out_dtype, tm, tn, name):
    r, k = x.shape
    n = ws[0].shape[1]
    in_specs = [pl.BlockSpec((tm, k), lambda j, i: (i, 0))]
    in_specs += [pl.BlockSpec((k, tn), lambda j, i: (0, j)) for _ in ws]
    args = [x] + list(ws)
    for arr, off in extras:
        if arr.shape[0] == 1:
            in_specs.append(pl.BlockSpec((1, tn), lambda j, i, off=off: (0, j + off)))
        else:
            in_specs.append(pl.BlockSpec((tm, tn), lambda j, i, off=off: (i, j + off)))
        args.append(arr)
    return pl.pallas_call(
        functools.partial(_mm_kernel, n_w=len(ws), epilogue=epilogue),
        grid=(n // tn, r // tm),
        in_specs=in_specs,
        out_specs=pl.BlockSpec((tm, tn), lambda j, i: (i, j)),
        out_shape=jax.ShapeDtypeStruct((r, n), out_dtype),
        compiler_params=pltpu.CompilerParams(
            dimension_semantics=("arbitrary", "arbitrary"), vmem_limit_bytes=VMEM_LIMIT),
        name=name,
    )(*args)


def _mm_k2_kernel(x_ref, w_ref, gate_ref, add_ref, o_ref, acc_ref):
    kk = pl.program_id(2)
    part = jnp.dot(x_ref[...], w_ref[...], preferred_element_type=F32)

    @pl.when(kk == 0)
    def _():
        acc_ref[...] = part

    @pl.when(kk == 1)
    def _():
        y = acc_ref[...] + part
        o_ref[...] = (y * gate_ref[...].astype(F32) + add_ref[...].astype(F32)).astype(o_ref.dtype)


def _matmul_k2_gate_add(x, w, gates, gate_off, add, out_dtype, tm, tn, name):
    r, k = x.shape
    n = w.shape[1]
    kh = k // 2
    return pl.pallas_call(
        _mm_k2_kernel,
        grid=(n // tn, r // tm, 2),
        in_specs=[
            pl.BlockSpec((tm, kh), lambda j, i, kk: (i, kk)),
            pl.BlockSpec((kh, tn), lambda j, i, kk: (kk, j)),
            pl.BlockSpec((tm, tn), lambda j, i, kk: (i, j + gate_off)),
            pl.BlockSpec((tm, tn), lambda j, i, kk: (i, j)),
        ],
        out_specs=pl.BlockSpec((tm, tn), lambda j, i, kk: (i, j)),
        out_shape=jax.ShapeDtypeStruct((r, n), out_dtype),
        scratch_shapes=[pltpu.VMEM((tm, tn), F32)],
        compiler_params=pltpu.CompilerParams(
            dimension_semantics=("arbitrary", "arbitrary", "arbitrary"), vmem_limit_bytes=VMEM_LIMIT),
        name=name,
    )(x, w, gates, add)


def _mm_conv_kernel(*refs, n_w, taps, halo, tm, tn, n_row_tiles, n_tiles, tiles_per_seq, act, row_group):
    x_ref = refs[0]
    w_refs = refs[1:1 + n_w]
    cw_ref, cb_ref, o_ref, vbuf, stage = refs[1 + n_w:]
    n = pl.program_id(0)
    stride = tm // SUBLANES

    @pl.when(n == 0)
    def _():
        vbuf[...] = jnp.zeros(vbuf.shape, F32)

    cur = jnp.minimum(n, n_tiles - 1)
    starts_seq = lax.rem(lax.rem(cur, n_row_tiles), tiles_per_seq) == 0
    slabs_per_group = MXU_COLS // LANES

    def column_group(g, carry):
        col0 = pl.multiple_of(g * MXU_COLS, MXU_COLS)
        for s2 in range(slabs_per_group):
            s = g * slabs_per_group + s2
            lanes = pl.ds(pl.multiple_of(col0 + s2 * LANES, LANES), LANES)
            for i0 in range(0, stride, row_group):
                accs = [cb_ref[:, lanes]] * row_group
                for k in range(taps):
                    wk = cw_ref[k * SUBLANES:(k + 1) * SUBLANES, lanes]
                    for ii in range(row_group):
                        start = halo + i0 + ii - (taps - 1) + k
                        accs[ii] = accs[ii] + vbuf[s, pl.ds(start, SUBLANES, stride=stride), :] * wk
                for ii in range(row_group):
                    stage[s, pl.ds(i0 + ii, SUBLANES, stride=stride), :] = act(accs[ii])
            o_ref[:, lanes] = stage[s].astype(o_ref.dtype)

        x = x_ref[...]
        cols = pl.ds(col0, MXU_COLS)
        accs = [jnp.dot(x, w[:, cols], preferred_element_type=F32) for w in w_refs]
        v = accs[0] * _sigmoid(accs[1]) if n_w == 2 else accs[0]
        for s2 in range(slabs_per_group):
            s = g * slabs_per_group + s2
            vbuf[s, 0:halo, :] = jnp.where(starts_seq, 0.0, vbuf[s, tm:tm + halo, :])
            vbuf[s, halo:halo + tm, :] = v[:, s2 * LANES:(s2 + 1) * LANES]
        return carry

    lax.fori_loop(0, tn // MXU_COLS, column_group, 0)


def _matmul_conv(x, ws, conv_w, conv_b, act, tm, tn, tiles_per_seq, name):
    r, k = x.shape
    n_cols = ws[0].shape[1]
    taps = conv_w.shape[0]
    halo = -(-(taps - 1) // SUBLANES) * SUBLANES
    n_row_tiles = r // tm
    n_tiles = n_row_tiles * (n_cols // tn)
    assert tm % (2 * SUBLANES) == 0 and tn % MXU_COLS == 0
    row_group = 12
    assert (tm // SUBLANES) % row_group == 0
    rep = lambda a: jnp.broadcast_to(a[:, None, :], (a.shape[0], SUBLANES, n_cols)).reshape(-1, n_cols)

    def cur_tile(n):
        c = jnp.minimum(n, n_tiles - 1)
        return c % n_row_tiles, c // n_row_tiles

    def lag_tile(n):
        c = jnp.maximum(n - 1, 0)
        return c % n_row_tiles, c // n_row_tiles

    in_specs = [pl.BlockSpec((tm, k), lambda n: (cur_tile(n)[0], 0))]
    in_specs += [pl.BlockSpec((k, tn), lambda n: (0, cur_tile(n)[1])) for _ in ws]
    in_specs += [pl.BlockSpec((taps * SUBLANES, tn), lambda n: (0, lag_tile(n)[1])),
                 pl.BlockSpec((SUBLANES, tn), lambda n: (0, lag_tile(n)[1]))]
    return pl.pallas_call(
        functools.partial(_mm_conv_kernel, n_w=len(ws), taps=taps, halo=halo, tm=tm, tn=tn,
                          n_row_tiles=n_row_tiles, n_tiles=n_tiles, tiles_per_seq=tiles_per_seq, act=act,
                          row_group=row_group),
        grid=(n_tiles + 1,),
        in_specs=in_specs,
        out_specs=pl.BlockSpec((tm, tn), lambda n: lag_tile(n)),
        out_shape=jax.ShapeDtypeStruct((r, n_cols), BF16),
        scratch_shapes=[pltpu.VMEM((tn // LANES, halo + tm, LANES), F32),
                        pltpu.VMEM((tn // LANES, tm, LANES), F32)],
        compiler_params=pltpu.CompilerParams(
            dimension_semantics=("arbitrary",), vmem_limit_bytes=VMEM_LIMIT),
        name=name,
    )(x, *ws, rep(conv_w), rep(conv_b.reshape(1, n_cols)))


def _dt_kernel(x_ref, w_ref, bias_ref, o_ref, ot_ref, *, tm, t_rows):
    i = pl.program_id(0)
    raw = jnp.dot(x_ref[...], w_ref[...], preferred_element_type=F32)
    dt = _softplus(raw + bias_ref[...])
    tiles_per_seq = t_rows // tm
    row_in_seq = lax.rem(i, tiles_per_seq) * tm + lax.broadcasted_iota(jnp.int32, dt.shape, 0)
    dt = jnp.where(row_in_seq < PAD, 0.0, dt)
    o_ref[...] = dt
    ot_ref[...] = dt.T


def _dt_proj(h2, w_dt, dt_bias, tm, t_rows):
    r, k = h2.shape
    nh = w_dt.shape[1]
    assert t_rows % tm == 0 and tm % LANES == 0
    return pl.pallas_call(
        functools.partial(_dt_kernel, tm=tm, t_rows=t_rows),
        grid=(r // tm,),
        in_specs=[
            pl.BlockSpec((tm, k), lambda i: (i, 0)),
            pl.BlockSpec((k, nh), lambda i: (0, 0)),
            pl.BlockSpec((1, nh), lambda i: (0, 0)),
        ],
        out_specs=[pl.BlockSpec((tm, nh), lambda i: (i, 0)),
                   pl.BlockSpec((nh, tm), lambda i: (0, i))],
        out_shape=[jax.ShapeDtypeStruct((r, nh), F32), jax.ShapeDtypeStruct((nh, r), F32)],
        compiler_params=pltpu.CompilerParams(
            dimension_semantics=("arbitrary",), vmem_limit_bytes=VMEM_LIMIT),
        name="dt_proj",
    )(h2, w_dt, dt_bias.reshape(1, nh))


def _ln_act_kernel(cv_ref, sg_ref, g_ref, b_ref, o_ref):
    y = _layer_norm_rows(cv_ref[...].astype(F32), g_ref[...], b_ref[...])
    o_ref[...] = (_silu(y) * sg_ref[...].astype(F32)).astype(o_ref.dtype)


def _ln_act(cv, sg, ln_g, ln_b, tl):
    r, c = cv.shape
    vec = pl.BlockSpec((1, c), lambda i: (0, 0))
    tile = pl.BlockSpec((tl, c), lambda i: (i, 0))
    return pl.pallas_call(
        _ln_act_kernel,
        grid=(r // tl,),
        in_specs=[tile, tile, vec, vec],
        out_specs=tile,
        out_shape=jax.ShapeDtypeStruct((r, c), BF16),
        compiler_params=pltpu.CompilerParams(
            dimension_semantics=("arbitrary",), vmem_limit_bytes=VMEM_LIMIT),
        name="conv_ln_act",
    )(cv, sg, ln_g.reshape(1, c), ln_b.reshape(1, c))


def _ssd_kernel(xbc_ref, dt_ref, dtt_ref, sz_ref, alog_ref, alogt_ref, dskip_ref, ng_ref,
                o_ref, state, *, inner):
    c = pl.program_id(1)
    q = CHUNK
    n = SSD_STATE
    n_groups = state.shape[0]
    gw = inner // n_groups
    pair_w = 2 * SSD_HEAD_DIM
    n_pairs = gw // pair_w
    heads_per_group = gw // SSD_HEAD_DIM

    @pl.when(c == 0)
    def _():
        state[...] = jnp.zeros(state.shape, F32)

    dt = dt_ref[0]
    dtt = dtt_ref[...]
    a_row = -jnp.exp(alog_ref[...])
    a_col = -jnp.exp(alogt_ref[...])

    rows = lax.broadcasted_iota(jnp.int32, (q, q), 0)
    cols = lax.broadcasted_iota(jnp.int32, (q, q), 1)
    causal = cols <= rows
    tri = jnp.where(causal, 1.0, 0.0).astype(F32)
    tri_t = jnp.where(rows <= cols, 1.0, 0.0).astype(F32)
    a_cs = jnp.dot(tri, dt * a_row, preferred_element_type=F32,
                   precision=lax.Precision.HIGHEST)
    a_cs_t = jnp.dot(dtt * a_col, tri_t, preferred_element_type=F32,
                     precision=lax.Precision.HIGHEST)
    w_state_t = dtt * jnp.exp(a_cs_t[:, q - 1:q] - a_cs_t)

    lane = lax.broadcasted_iota(jnp.int32, (q, pair_w), 1)
    first = lane < SSD_HEAD_DIM

    for g in range(n_groups):
        bm_b = xbc_ref[0, :, inner + g * n:inner + (g + 1) * n]
        cm_b = xbc_ref[0, :, inner + (n_groups + g) * n:inner + (n_groups + g + 1) * n]
        cb = lax.dot_general(cm_b, bm_b, (((1,), (1,)), ((), ())), preferred_element_type=F32)
        bm_t = bm_b.astype(F32).T

        y_parts = []
        for j in range(n_pairs):
            h0 = g * heads_per_group + 2 * j
            h1 = h0 + 1
            lo, hi = g * gw + j * pair_w, g * gw + (j + 1) * pair_w
            ac0 = jnp.broadcast_to(a_cs[:, h0:h0 + 1], (q, q))
            ac1 = jnp.broadcast_to(a_cs[:, h1:h1 + 1], (q, q))
            ac_pair = jnp.where(first, ac0, ac1)
            tot_pair = ac_pair[q - 1:q, :]

            x_b = xbc_ref[0, :, lo:hi]
            zero = jnp.zeros_like(x_b)
            x_bd = jnp.concatenate([jnp.where(first, x_b, zero), jnp.where(first, zero, x_b)], axis=0)

            seg0 = jnp.where(causal, ac0 - a_cs_t[h0:h0 + 1, :], -jnp.inf)
            seg1 = jnp.where(causal, ac1 - a_cs_t[h1:h1 + 1, :], -jnp.inf)
            m0 = (cb * dtt[h0:h0 + 1, :] * jnp.exp(seg0)).astype(BF16)
            m1 = (cb * dtt[h1:h1 + 1, :] * jnp.exp(seg1)).astype(BF16)
            y_diag = jnp.dot(jnp.concatenate([m0, m1], axis=1), x_bd, preferred_element_type=F32)

            st = state[g, :, j * pair_w:(j + 1) * pair_w]
            y_off = jnp.dot(cm_b, st.astype(BF16), preferred_element_type=F32) * jnp.exp(ac_pair)
            y_parts.append(y_diag + y_off + dskip_ref[:, lo:hi] * x_b.astype(F32))

            w0 = (bm_t * w_state_t[h0:h0 + 1, :]).astype(BF16)
            w1 = (bm_t * w_state_t[h1:h1 + 1, :]).astype(BF16)
            upd = jnp.dot(jnp.concatenate([w0, w1], axis=1), x_bd, preferred_element_type=F32)
            state[g, :, j * pair_w:(j + 1) * pair_w] = st * jnp.exp(tot_pair) + upd

        y = jnp.concatenate(y_parts, axis=1)
        yz = y * sz_ref[0, :, g * gw:(g + 1) * gw].astype(F32)
        ms = jnp.mean(yz * yz, axis=-1, keepdims=True)
        o_ref[0, :, g * gw:(g + 1) * gw] = (
            yz * lax.rsqrt(ms + RMS_EPS) * ng_ref[:, g * gw:(g + 1) * gw]).astype(o_ref.dtype)


def _ssd_branch(xbc3, dt3, dt_t, sz3, a_log, d_skip, norm_g):
    nb, t_rows, xbc_w = xbc3.shape
    inner = sz3.shape[-1]
    heads = a_log.shape[0]
    nc = t_rows // CHUNK
    dskip_x = jnp.repeat(d_skip, SSD_HEAD_DIM).reshape(1, inner)
    full = lambda shape: pl.BlockSpec(shape, lambda bi, ci: (0,) * len(shape))

    return pl.pallas_call(
        functools.partial(_ssd_kernel, inner=inner),
        grid=(nb, nc),
        in_specs=[
            pl.BlockSpec((1, CHUNK, xbc_w), lambda bi, ci: (bi, ci, 0)),
            pl.BlockSpec((1, CHUNK, heads), lambda bi, ci: (bi, ci, 0)),
            pl.BlockSpec((heads, CHUNK), lambda bi, ci: (0, bi * nc + ci)),
            pl.BlockSpec((1, CHUNK, inner), lambda bi, ci: (bi, ci, 0)),
            full((1, heads)), full((heads, 1)), full((1, inner)), full((1, inner)),
        ],
        out_specs=pl.BlockSpec((1, CHUNK, inner), lambda bi, ci: (bi, ci, 0)),
        out_shape=jax.ShapeDtypeStruct((nb, t_rows, inner), BF16),
        scratch_shapes=[pltpu.VMEM((SSD_GROUPS, SSD_STATE, inner // SSD_GROUPS), F32)],
        compiler_params=pltpu.CompilerParams(
            dimension_semantics=("arbitrary", "arbitrary"), vmem_limit_bytes=VMEM_LIMIT),
        name="ssd_branch",
    )(xbc3, dt3, dt_t, sz3, a_log.reshape(1, heads), a_log.reshape(heads, 1), dskip_x,
      norm_g.reshape(1, inner))


def _ln_out_kernel(x_ref, sub_ref, gi_ref, bi_ref, go_ref, bo_ref, o_ref):
    h = _layer_norm_rows(x_ref[0], gi_ref[...], bi_ref[...])
    o_ref[0] = _layer_norm_rows(DEEPNORM_ALPHA * h + sub_ref[0].astype(F32), go_ref[...], bo_ref[...])


def _ln_out(x, sub3, gi, bi, go, bo):
    nb, seq, d = x.shape
    row = lambda a: a.reshape(1, d)
    vec = pl.BlockSpec((1, d), lambda b_, t: (0, 0))
    return pl.pallas_call(
        _ln_out_kernel,
        grid=(nb, seq // CHUNK),
        in_specs=[
            pl.BlockSpec((1, CHUNK, d), lambda b_, t: (b_, t, 0)),
            pl.BlockSpec((1, CHUNK, d), lambda b_, t: (b_, t + 1, 0)),
            vec, vec, vec, vec,
        ],
        out_specs=pl.BlockSpec((1, CHUNK, d), lambda b_, t: (b_, t, 0)),
        out_shape=jax.ShapeDtypeStruct((nb, seq, d), F32),
        name="ln_out",
    )(x, sub3, row(gi), row(bi), row(go), row(bo))


def kernel(x, meta_tokens, ln_in_g, ln_in_b, w_in, b_gate, conv_w, conv_b, conv_ln_g, conv_ln_b,
           w_conv_out, ssd_conv_w, ssd_conv_b, dt_bias, a_log, d_skip, ssd_norm_g, w_ssd_out,
           w_out, ln_out_g, ln_out_b):
    nb, seq, d = x.shape
    assert w_in.shape[0] == 1, "single-layer trunk"
    assert seq % CHUNK == 0
    t_rows = seq + CHUNK
    r = nb * t_rows
    inner = w_ssd_out.shape[1]
    heads = a_log.shape[1]
    xbc_w = ssd_conv_w.shape[2]
    tiles_per_seq = 4
    tm = t_rows // tiles_per_seq
    tn = 1024
    assert t_rows % tiles_per_seq == 0 and tm % 16 == 0

    o_val, o_glu, o_gate, o_z = 0, d, 2 * d, 3 * d
    o_xbc = o_z + inner
    o_dt = o_xbc + xbc_w
    o_g = o_dt + heads
    w = w_in[0]
    cut = lambda a, b_: w[:, a:b_].astype(BF16)

    h3 = _ln_in(x, meta_tokens, ln_in_g, ln_in_b)
    h2 = h3.reshape(r, d)

    cv = _matmul_conv(h2, [cut(o_val, o_glu), cut(o_glu, o_gate)], conv_w[0], conv_b[0],
                      lambda a: a, tm, tn // 2, tiles_per_seq, "in_proj_glu_conv")
    sg = _matmul(h2, [cut(o_gate, o_z)], [], lambda a, e: _silu(a[0]), BF16, tm, tn, "in_proj_gate")
    sz = _matmul(h2, [cut(o_z, o_xbc)], [], lambda a, e: _silu(a[0]), BF16, tm, tn, "in_proj_z")
    xbc = _matmul_conv(h2, [cut(o_xbc, o_dt)], ssd_conv_w[0], ssd_conv_b[0],
                       _silu, tm, tn, tiles_per_seq, "in_proj_xbc_conv")
    dt, dt_t = _dt_proj(h2, cut(o_dt, o_g), dt_bias[0], t_rows // 3, t_rows)
    gates = _matmul(h2, [cut(o_g, o_g + 2 * d)], [(b_gate[0].reshape(1, 2 * d), 0)],
                    lambda a, e: _sigmoid(a[0] + e[0]), BF16, tm, tn, "in_proj_merge_gates")

    u = _ln_act(cv, sg, conv_ln_g[0], conv_ln_b[0], tm // 6)
    yn = _ssd_branch(xbc.reshape(nb, t_rows, xbc_w), dt.reshape(nb, t_rows, heads), dt_t,
                     sz.reshape(nb, t_rows, inner), a_log[0], d_skip[0], ssd_norm_g[0])

    gated_conv = _matmul(u, [w_conv_out[0].astype(BF16)], [(gates, 0)],
                         lambda a, e: a[0] * e[0].astype(F32), BF16, tm, tn, "conv_out_proj")
    merged = _matmul_k2_gate_add(yn.reshape(r, inner), w_ssd_out[0].astype(BF16), gates, 2 * d // tn,
                                 gated_conv, BF16, tm, tn // 2, "ssd_out_proj_merge")
    sub = _matmul(merged, [w_out[0].astype(BF16)], [], lambda a, e: a[0], F32, tm, tn, "out_proj")

    return _ln_out(x, sub.reshape(nb, t_rows, d), ln_in_g, ln_in_b, ln_out_g[0], ln_out_b[0])
```

```python
import functools

import jax
import jax.numpy as jnp
from jax import lax
from jax.experimental import pallas as pl
from jax.experimental.pallas import tpu as pltpu

F32 = jnp.float32
BF16 = jnp.bfloat16

N_META = 16
CHUNK = 128
PAD = CHUNK - N_META
CONV_WIDTH = 31
SSD_HEAD_DIM = 64
SSD_GROUPS = 8
SSD_STATE = 128
SSD_CONV_WIDTH = 4
LN_EPS = 1e-5
RMS_EPS = 1e-5
DEEPNORM_ALPHA = 2.0 ** 0.25

SUBLANES = 8
LANES = 128
MXU_COLS = 256
VMEM_LIMIT = 56 * 1024 * 1024


def _sigmoid(x):
    return 1.0 / (1.0 + jnp.exp(-x))


def _silu(x):
    return x * _sigmoid(x)


def _softplus(x):
    return jnp.maximum(x, 0.0) + jnp.log1p(jnp.exp(-jnp.abs(x)))


def _layer_norm_rows(v, g, b):
    mu = jnp.mean(v, axis=-1, keepdims=True)
    d = v - mu
    var = jnp.mean(d * d, axis=-1, keepdims=True)
    return d * lax.rsqrt(var + LN_EPS) * g + b


def _ln_in_kernel(x_ref, meta_ref, g_ref, b_ref, o_ref):
    t = pl.program_id(1)
    g = g_ref[...]
    b = b_ref[...]

    @pl.when(t == 0)
    def _():
        d = o_ref.shape[-1]
        o_ref[0, :PAD, :] = jnp.zeros((PAD, d), o_ref.dtype)
        o_ref[0, PAD:, :] = _layer_norm_rows(meta_ref[...], g, b).astype(o_ref.dtype)

    @pl.when(t > 0)
    def _():
        o_ref[0] = _layer_norm_rows(x_ref[0], g, b).astype(o_ref.dtype)


def _ln_in(x, meta, g, b):
    nb, seq, d = x.shape
    nt = seq // CHUNK + 1
    return pl.pallas_call(
        _ln_in_kernel,
        grid=(nb, nt),
        in_specs=[
            pl.BlockSpec((1, CHUNK, d), lambda bi, t: (bi, jnp.maximum(t - 1, 0), 0)),
            pl.BlockSpec((N_META, d), lambda bi, t: (0, 0)),
            pl.BlockSpec((1, d), lambda bi, t: (0, 0)),
            pl.BlockSpec((1, d), lambda bi, t: (0, 0)),
        ],
        out_specs=pl.BlockSpec((1, CHUNK, d), lambda bi, t: (bi, t, 0)),
        out_shape=jax.ShapeDtypeStruct((nb, nt * CHUNK, d), BF16),
        name="ln_in",
    )(x, meta, g.reshape(1, d), b.reshape(1, d))


def _cast_weight_chunk(wc_refs, wb_refs, slot, chunk, kc):
    rows = pl.ds(pl.multiple_of(chunk * kc, kc), kc)
    for wc, wb in zip(wc_refs, wb_refs):
        wb[slot, rows, :] = wc[...].astype(BF16)


def _mm_kernel(*refs, n_w, n_e, epilogue, kc):
    x_ref = refs[0]
    wc_refs = refs[1:1 + n_w]
    e_refs = refs[1 + n_w:1 + n_w + n_e]
    o_ref = refs[1 + n_w + n_e]
    wb_refs = refs[2 + n_w + n_e:]
    jj = pl.program_id(0)
    i = pl.program_id(1)
    n_col_tiles = pl.num_programs(0) - 1

    @pl.when(jj < n_col_tiles)
    def _():
        _cast_weight_chunk(wc_refs, wb_refs, lax.rem(jj, 2), i, kc)

    @pl.when(jj > 0)
    def _():
        slot = lax.rem(jj - 1, 2)
        x = x_ref[...]
        accs = [jnp.dot(x, wb[slot], preferred_element_type=F32) for wb in wb_refs]
        o_ref[...] = epilogue(accs, [e[...] for e in e_refs]).astype(o_ref.dtype)


def _matmul(x, ws, extras, epilogue, out_dtype, tm, tn, name):
    r, k = x.shape
    n = ws[0][2]
    nj, ni = n // tn, r // tm
    kc = k // ni
    assert k % ni == 0 and kc % 16 == 0 and n % tn == 0 and r % tm == 0
    assert all(c0 % LANES == 0 for _, c0, _ in ws)
    row = lambda jj, i: jnp.where(jj == 0, 0, i)
    col = lambda jj: jnp.maximum(jj - 1, 0)
    in_specs = [pl.BlockSpec((tm, k), lambda jj, i: (row(jj, i), 0))]
    in_specs += [pl.BlockSpec((pl.Element(kc), pl.Element(tn)),
                              lambda jj, i, c0=c0: (
                                  i * kc, (c0 // LANES + jnp.minimum(jj, nj - 1) * (tn // LANES)) * LANES))
                 for _, c0, _ in ws]
    args = [x] + [w for w, _, _ in ws]
    for arr, off in extras:
        if arr.shape[0] == 1:
            in_specs.append(pl.BlockSpec((1, tn), lambda jj, i, off=off: (0, col(jj) + off)))
        else:
            in_specs.append(pl.BlockSpec((tm, tn), lambda jj, i, off=off: (row(jj, i), col(jj) + off)))
        args.append(arr)
    return pl.pallas_call(
        functools.partial(_mm_kernel, n_w=len(ws), n_e=len(extras), epilogue=epilogue, kc=kc),
        grid=(nj + 1, ni),
        in_specs=in_specs,
        out_specs=pl.BlockSpec((tm, tn), lambda jj, i: (row(jj, i), col(jj))),
        out_shape=jax.ShapeDtypeStruct((r, n), out_dtype),
        scratch_shapes=[pltpu.VMEM((2, k, tn), BF16) for _ in ws],
        compiler_params=pltpu.CompilerParams(
            dimension_semantics=("arbitrary", "arbitrary"), vmem_limit_bytes=VMEM_LIMIT),
        name=name,
    )(*args)


def _mm_conv_kernel(*refs, n_w, taps, halo, tm, tn, n_row_tiles, n_tiles, tiles_per_seq, act, row_group, kc):
    x_ref = refs[0]
    wc_refs = refs[1:1 + n_w]
    cw_ref, cb_ref, o_ref = refs[1 + n_w:4 + n_w]
    wb_refs = refs[4 + n_w:4 + 2 * n_w]
    vbuf, stage = refs[4 + 2 * n_w:]
    step = pl.program_id(0)
    m = step - n_row_tiles
    stride = tm // SUBLANES
    slabs_per_group = MXU_COLS // LANES
    n_groups = tn // MXU_COLS

    @pl.when(step == 0)
    def _():
        vbuf[...] = jnp.zeros(vbuf.shape, F32)

    @pl.when(step < n_tiles)
    def _():
        _cast_weight_chunk(wc_refs, wb_refs, lax.rem(step // n_row_tiles, 2), lax.rem(step, n_row_tiles), kc)

    @pl.when(m >= 0)
    def _():
        cur = jnp.minimum(m, n_tiles - 1)
        slot = lax.rem(cur // n_row_tiles, 2)
        starts_seq = lax.rem(lax.rem(cur, n_row_tiles), tiles_per_seq) == 0

        def column_group(g, carry):
            col0 = pl.multiple_of(g * MXU_COLS, MXU_COLS)
            for s2 in range(slabs_per_group):
                s = g * slabs_per_group + s2
                lanes = pl.ds(pl.multiple_of(col0 + s2 * LANES, LANES), LANES)
                for i0 in range(0, stride, row_group):
                    accs = [cb_ref[:, lanes]] * row_group
                    for k in range(taps):
                        wk = cw_ref[k * SUBLANES:(k + 1) * SUBLANES, lanes]
                        for ii in range(row_group):
                            start = halo + i0 + ii - (taps - 1) + k
                            accs[ii] = accs[ii] + vbuf[s, pl.ds(start, SUBLANES, stride=stride), :] * wk
                    for ii in range(row_group):
                        stage[s, pl.ds(i0 + ii, SUBLANES, stride=stride), :] = act(accs[ii])
                o_ref[:, lanes] = stage[s].astype(o_ref.dtype)

            x = x_ref[...]
            cols = pl.ds(col0, MXU_COLS)
            accs = [jnp.dot(x, wb[slot, :, cols], preferred_element_type=F32) for wb in wb_refs]
            v = accs[0] * _sigmoid(accs[1]) if n_w == 2 else accs[0]
            for s2 in range(slabs_per_group):
                s = g * slabs_per_group + s2
                vbuf[s, 0:halo, :] = jnp.where(starts_seq, 0.0, vbuf[s, tm:tm + halo, :])
                vbuf[s, halo:halo + tm, :] = v[:, s2 * LANES:(s2 + 1) * LANES]
            return carry
        lax.fori_loop(0, n_groups, column_group, 0)


def _matmul_conv(x, ws, conv_w, conv_b, act, tm, tn, tiles_per_seq, name):
    r, k = x.shape
    n_cols = ws[0][2]
    taps = conv_w.shape[0]
    halo = -(-(taps - 1) // SUBLANES) * SUBLANES
    n_row_tiles = r // tm
    n_col_tiles = n_cols // tn
    n_tiles = n_row_tiles * n_col_tiles
    kc = k // n_row_tiles
    assert tm % (2 * SUBLANES) == 0 and tn % MXU_COLS == 0 and k % n_row_tiles == 0 and kc % 16 == 0
    assert all(c0 % LANES == 0 for _, c0, _ in ws)
    row_group = 12
    assert (tm // SUBLANES) % row_group == 0
    rep = lambda a: jnp.broadcast_to(a[:, None, :], (a.shape[0], SUBLANES, n_cols)).reshape(-1, n_cols)

    def cur_tile(step):
        c = jnp.clip(step - n_row_tiles, 0, n_tiles - 1)
        return c % n_row_tiles, c // n_row_tiles

    def lag_tile(step):
        c = jnp.clip(step - n_row_tiles - 1, 0, n_tiles - 1)
        return c % n_row_tiles, c // n_row_tiles

    def chunk(step, c0):
        tile_col = jnp.minimum(step // n_row_tiles, n_col_tiles - 1)
        return (step % n_row_tiles) * kc, (c0 // LANES + tile_col * (tn // LANES)) * LANES

    in_specs = [pl.BlockSpec((tm, k), lambda n: (cur_tile(n)[0], 0))]
    in_specs += [pl.BlockSpec((pl.Element(kc), pl.Element(tn)), lambda n, c0=c0: chunk(n, c0))
                 for _, c0, _ in ws]
    in_specs += [pl.BlockSpec((taps * SUBLANES, tn), lambda n: (0, lag_tile(n)[1])),
                 pl.BlockSpec((SUBLANES, tn), lambda n: (0, lag_tile(n)[1]))]
    return pl.pallas_call(
        functools.partial(_mm_conv_kernel, n_w=len(ws), taps=taps, halo=halo, tm=tm, tn=tn,
                          n_row_tiles=n_row_tiles, n_tiles=n_tiles, tiles_per_seq=tiles_per_seq, act=act,
                          row_group=row_group, kc=kc),
        grid=(n_row_tiles + n_tiles + 1,),
        in_specs=in_specs,
        out_specs=pl.BlockSpec((tm, tn), lambda n: lag_tile(n)),
        out_shape=jax.ShapeDtypeStruct((r, n_cols), BF16),
        scratch_shapes=[pltpu.VMEM((2, k, tn), BF16) for _ in ws] + [
            pltpu.VMEM((tn // LANES, halo + tm, LANES), F32),
            pltpu.VMEM((tn // LANES, tm, LANES), F32)],
        compiler_params=pltpu.CompilerParams(
            dimension_semantics=("arbitrary",), vmem_limit_bytes=VMEM_LIMIT),
        name=name,
    )(x, *[w for w, _, _ in ws], rep(conv_w), rep(conv_b.reshape(1, n_cols)))


def _dt_kernel(x_ref, w_ref, bias_ref, o_ref, ot_ref, *, tm, t_rows):
    i = pl.program_id(0)
    raw = jnp.dot(x_ref[...], w_ref[...], preferred_element_type=F32)
    dt = _softplus(raw + bias_ref[...])
    tiles_per_seq = t_rows // tm
    row_in_seq = lax.rem(i, tiles_per_seq) * tm + lax.broadcasted_iota(jnp.int32, dt.shape, 0)
    dt = jnp.where(row_in_seq < PAD, 0.0, dt)
    o_ref[...] = dt
    ot_ref[...] = dt.T


def _dt_proj(h2, w_dt, dt_bias, tm, t_rows):
    r, k = h2.shape
    nh = w_dt.shape[1]
    assert t_rows % tm == 0 and tm % LANES == 0
    return pl.pallas_call(
        functools.partial(_dt_kernel, tm=tm, t_rows=t_rows),
        grid=(r // tm,),
        in_specs=[
            pl.BlockSpec((tm, k), lambda i: (i, 0)),
            pl.BlockSpec((k, nh), lambda i: (0, 0)),
            pl.BlockSpec((1, nh), lambda i: (0, 0)),
        ],
        out_specs=[pl.BlockSpec((tm, nh), lambda i: (i, 0)),
                   pl.BlockSpec((nh, tm), lambda i: (0, i))],
        out_shape=[jax.ShapeDtypeStruct((r, nh), F32), jax.ShapeDtypeStruct((nh, r), F32)],
        compiler_params=pltpu.CompilerParams(
            dimension_semantics=("arbitrary",), vmem_limit_bytes=VMEM_LIMIT),
        name="dt_proj",
    )(h2, w_dt, dt_bias.reshape(1, nh))


def _ln_act_kernel(cv_ref, sg_ref, g_ref, b_ref, o_ref):
    y = _layer_norm_rows(cv_ref[...].astype(F32), g_ref[...], b_ref[...])
    o_ref[...] = (_silu(y) * sg_ref[...].astype(F32)).astype(o_ref.dtype)


def _ln_act(cv, sg, ln_g, ln_b, tl):
    r, c = cv.shape
    vec = pl.BlockSpec((1, c), lambda i: (0, 0))
    tile = pl.BlockSpec((tl, c), lambda i: (i, 0))
    return pl.pallas_call(
        _ln_act_kernel,
        grid=(r // tl,),
        in_specs=[tile, tile, vec, vec],
        out_specs=tile,
        out_shape=jax.ShapeDtypeStruct((r, c), BF16),
        compiler_params=pltpu.CompilerParams(
            dimension_semantics=("arbitrary",), vmem_limit_bytes=VMEM_LIMIT),
        name="conv_ln_act",
    )(cv, sg, ln_g.reshape(1, c), ln_b.reshape(1, c))


def _ssd_kernel(xbc_ref, dt_ref, dtt_ref, sz_ref, alog_ref, alogt_ref, dskip_ref, ng_ref,
                o_ref, state, *, inner):
    c = pl.program_id(1)
    q = CHUNK
    n = SSD_STATE
    n_groups = state.shape[0]
    gw = inner // n_groups
    pair_w = 2 * SSD_HEAD_DIM
    n_pairs = gw // pair_w
    heads_per_group = gw // SSD_HEAD_DIM

    @pl.when(c == 0)
    def _():
        state[...] = jnp.zeros(state.shape, F32)

    dt = dt_ref[0]
    dtt = dtt_ref[...]
    a_row = -jnp.exp(alog_ref[...])
    a_col = -jnp.exp(alogt_ref[...])

    rows = lax.broadcasted_iota(jnp.int32, (q, q), 0)
    cols = lax.broadcasted_iota(jnp.int32, (q, q), 1)
    causal = cols <= rows
    tri = jnp.where(causal, 1.0, 0.0).astype(F32)
    tri_t = jnp.where(rows <= cols, 1.0, 0.0).astype(F32)
    a_cs = jnp.dot(tri, dt * a_row, preferred_element_type=F32,
                   precision=lax.Precision.HIGHEST)
    a_cs_t = jnp.dot(dtt * a_col, tri_t, preferred_element_type=F32,
                     precision=lax.Precision.HIGHEST)
    w_state_t = dtt * jnp.exp(a_cs_t[:, q - 1:q] - a_cs_t)

    lane = lax.broadcasted_iota(jnp.int32, (q, pair_w), 1)
    first = lane < SSD_HEAD_DIM

    for g in range(n_groups):
        bm_b = xbc_ref[0, :, inner + g * n:inner + (g + 1) * n]
        cm_b = xbc_ref[0, :, inner + (n_groups + g) * n:inner + (n_groups + g + 1) * n]
        cb = lax.dot_general(cm_b, bm_b, (((1,), (1,)), ((), ())), preferred_element_type=F32)
        bm_t = bm_b.astype(F32).T

        y_parts = []
        for j in range(n_pairs):
            h0 = g * heads_per_group + 2 * j
            h1 = h0 + 1
            lo, hi = g * gw + j * pair_w, g * gw + (j + 1) * pair_w
            ac0 = jnp.broadcast_to(a_cs[:, h0:h0 + 1], (q, q))
            ac1 = jnp.broadcast_to(a_cs[:, h1:h1 + 1], (q, q))
            ac_pair = jnp.where(first, ac0, ac1)
            tot_pair = ac_pair[q - 1:q, :]

            x_b = xbc_ref[0, :, lo:hi]
            zero = jnp.zeros_like(x_b)
            x_bd = jnp.concatenate([jnp.where(first, x_b, zero), jnp.where(first, zero, x_b)], axis=0)

            seg0 = jnp.where(causal, ac0 - a_cs_t[h0:h0 + 1, :], -jnp.inf)
            seg1 = jnp.where(causal, ac1 - a_cs_t[h1:h1 + 1, :], -jnp.inf)
            m0 = (cb * dtt[h0:h0 + 1, :] * jnp.exp(seg0)).astype(BF16)
            m1 = (cb * dtt[h1:h1 + 1, :] * jnp.exp(seg1)).astype(BF16)
            y_diag = jnp.dot(jnp.concatenate([m0, m1], axis=1), x_bd, preferred_element_type=F32)

            st = state[g, :, j * pair_w:(j + 1) * pair_w]
            y_off = jnp.dot(cm_b, st.astype(BF16), preferred_element_type=F32) * jnp.exp(ac_pair)
            y_parts.append(y_diag + y_off + dskip_ref[:, lo:hi] * x_b.astype(F32))

            w0 = (bm_t * w_state_t[h0:h0 + 1, :]).astype(BF16)
            w1 = (bm_t * w_state_t[h1:h1 + 1, :]).astype(BF16)
            upd = jnp.dot(jnp.concatenate([w0, w1], axis=1), x_bd, preferred_element_type=F32)
            state[g, :, j * pair_w:(j + 1) * pair_w] = st * jnp.exp(tot_pair) + upd

        y = jnp.concatenate(y_parts, axis=1)
        yz = y * sz_ref[0, :, g * gw:(g + 1) * gw].astype(F32)
        ms = jnp.mean(yz * yz, axis=-1, keepdims=True)
        o_ref[0, :, g * gw:(g + 1) * gw] = (
            yz * lax.rsqrt(ms + RMS_EPS) * ng_ref[:, g * gw:(g + 1) * gw]).astype(o_ref.dtype)


def _ssd_branch(xbc3, dt3, dt_t, sz3, a_log, d_skip, norm_g):
    nb, t_rows, xbc_w = xbc3.shape
    inner = sz3.shape[-1]
    heads = a_log.shape[0]
    nc = t_rows // CHUNK
    dskip_x = jnp.repeat(d_skip, SSD_HEAD_DIM).reshape(1, inner)
    full = lambda shape: pl.BlockSpec(shape, lambda bi, ci: (0,) * len(shape))

    return pl.pallas_call(
        functools.partial(_ssd_kernel, inner=inner),
        grid=(nb, nc),
        in_specs=[
            pl.BlockSpec((1, CHUNK, xbc_w), lambda bi, ci: (bi, ci, 0)),
            pl.BlockSpec((1, CHUNK, heads), lambda bi, ci: (bi, ci, 0)),
            pl.BlockSpec((heads, CHUNK), lambda bi, ci: (0, bi * nc + ci)),
            pl.BlockSpec((1, CHUNK, inner), lambda bi, ci: (bi, ci, 0)),
            full((1, heads)), full((heads, 1)), full((1, inner)), full((1, inner)),
        ],
        out_specs=pl.BlockSpec((1, CHUNK, inner), lambda bi, ci: (bi, ci, 0)),
        out_shape=jax.ShapeDtypeStruct((nb, t_rows, inner), BF16),
        scratch_shapes=[pltpu.VMEM((SSD_GROUPS, SSD_STATE, inner // SSD_GROUPS), F32)],
        compiler_params=pltpu.CompilerParams(
            dimension_semantics=("arbitrary", "arbitrary"), vmem_limit_bytes=VMEM_LIMIT),
        name="ssd_branch",
    )(xbc3, dt3, dt_t, sz3, a_log.reshape(1, heads), a_log.reshape(heads, 1), dskip_x,
      norm_g.reshape(1, inner))


def _ln_out_kernel(x_ref, sub_ref, gi_ref, bi_ref, go_ref, bo_ref, o_ref):
    h = _layer_norm_rows(x_ref[0], gi_ref[...], bi_ref[...])
    o_ref[0] = _layer_norm_rows(DEEPNORM_ALPHA * h + sub_ref[0].astype(F32), go_ref[...], bo_ref[...])


def _ln_out(x, sub3, gi, bi, go, bo):
    nb, seq, d = x.shape
    row = lambda a: a.reshape(1, d)
    vec = pl.BlockSpec((1, d), lambda b_, t: (0, 0))
    return pl.pallas_call(
        _ln_out_kernel,
        grid=(nb, seq // CHUNK),
        in_specs=[
            pl.BlockSpec((1, CHUNK, d), lambda b_, t: (b_, t, 0)),
            pl.BlockSpec((1, CHUNK, d), lambda b_, t: (b_, t + 1, 0)),
            vec, vec, vec, vec,
        ],
        out_specs=pl.BlockSpec((1, CHUNK, d), lambda b_, t: (b_, t, 0)),
        out_shape=jax.ShapeDtypeStruct((nb, seq, d), F32),
        name="ln_out",
    )(x, sub3, row(gi), row(bi), row(go), row(bo))


def kernel(x, meta_tokens, ln_in_g, ln_in_b, w_in, b_gate, conv_w, conv_b, conv_ln_g, conv_ln_b,
           w_conv_out, ssd_conv_w, ssd_conv_b, dt_bias, a_log, d_skip, ssd_norm_g, w_ssd_out,
           w_out, ln_out_g, ln_out_b):
    nb, seq, d = x.shape
    assert w_in.shape[0] == 1, "single-layer trunk"
    assert seq % CHUNK == 0
    t_rows = seq + CHUNK
    r = nb * t_rows
    inner = w_ssd_out.shape[1]
    heads = a_log.shape[1]
    xbc_w = ssd_conv_w.shape[2]
    tiles_per_seq = 4
    tm = t_rows // tiles_per_seq
    tn = 1024
    assert t_rows % tiles_per_seq == 0 and tm % 16 == 0

    o_val, o_glu, o_gate, o_z = 0, d, 2 * d, 3 * d
    o_xbc = o_z + inner
    o_dt = o_xbc + xbc_w
    o_g = o_dt + heads
    w = w_in[0]

    h3 = _ln_in(x, meta_tokens, ln_in_g, ln_in_b)
    h2 = h3.reshape(r, d)

    cv = _matmul_conv(h2, [(w, o_val, d), (w, o_glu, d)], conv_w[0], conv_b[0],
                      lambda a: a, tm, tn // 2, tiles_per_seq, "in_proj_glu_conv")
    sg = _matmul(h2, [(w, o_gate, d)], [], lambda a, e: _silu(a[0]), BF16, tm, tn, "in_proj_gate")
    sz = _matmul(h2, [(w, o_z, inner)], [], lambda a, e: _silu(a[0]), BF16, tm, tn, "in_proj_z")
    xbc = _matmul_conv(h2, [(w, o_xbc, xbc_w)], ssd_conv_w[0], ssd_conv_b[0],
                       _silu, tm, tn, tiles_per_seq, "in_proj_xbc_conv")
    dt, dt_t = _dt_proj(h2, w[:, o_dt:o_g].astype(BF16), dt_bias[0], t_rows // 3, t_rows)
    gates = _matmul(h2, [(w, o_g, 2 * d)], [(b_gate[0].reshape(1, 2 * d), 0)],
                    lambda a, e: _sigmoid(a[0] + e[0]), BF16, tm, tn, "in_proj_merge_gates")

    u = _ln_act(cv, sg, conv_ln_g[0], conv_ln_b[0], tm // 6)
    yn = _ssd_branch(xbc.reshape(nb, t_rows, xbc_w), dt.reshape(nb, t_rows, heads), dt_t,
                     sz.reshape(nb, t_rows, inner), a_log[0], d_skip[0], ssd_norm_g[0])

    gated_conv = _matmul(u, [(w_conv_out[0], 0, d)], [(gates, 0)],
                         lambda a, e: a[0] * e[0].astype(F32), BF16, tm, tn, "conv_out_proj")
    merged = _matmul(yn.reshape(r, inner), [(w_ssd_out[0], 0, d)],
                     [(gates, 2 * d // tn), (gated_conv, 0)],
                     lambda a, e: a[0] * e[0].astype(F32) + e[1].astype(F32),
                     BF16, tm // 2, tn // 2, "ssd_out_proj_merge")
    sub = _matmul(merged, [(w_out[0], 0, d)], [], lambda a, e: a[0], F32, tm, tn, "out_proj")

    return _ln_out(x, sub.reshape(nb, t_rows, d), ln_in_g, ln_in_b, ln_out_g[0], ln_out_b[0])
```

```python
import functools

import jax
import jax.numpy as jnp
from jax import lax
from jax.experimental import pallas as pl
from jax.experimental.pallas import tpu as pltpu

F32 = jnp.float32
BF16 = jnp.bfloat16

N_META = 16
CHUNK = 128
PAD = CHUNK - N_META
CONV_WIDTH = 31
SSD_HEAD_DIM = 64
SSD_GROUPS = 8
SSD_STATE = 128
SSD_CONV_WIDTH = 4
LN_EPS = 1e-5
RMS_EPS = 1e-5
DEEPNORM_ALPHA = 2.0 ** 0.25
LOG2_E = 1.4426950408889634

SUBLANES = 8
LANES = 128
MXU_COLS = 256
LN_ROWS = 16
VMEM_LIMIT = 56 * 1024 * 1024


def _sigmoid(x):
    return 1.0 / (1.0 + jnp.exp(-x))


def _silu(x):
    return x * _sigmoid(x)


def _softplus(x):
    return jnp.maximum(x, 0.0) + jnp.log1p(jnp.exp(-jnp.abs(x)))


def _layer_norm_rows(v, g, b):
    mu = jnp.mean(v, axis=-1, keepdims=True)
    d = v - mu
    var = jnp.mean(d * d, axis=-1, keepdims=True)
    return d * lax.rsqrt(var + LN_EPS) * g + b


def _ln_in_kernel(x_ref, meta_ref, g_ref, b_ref, o_ref):
    t = pl.program_id(1)
    g = g_ref[...]
    b = b_ref[...]

    @pl.when(t == 0)
    def _():
        d = o_ref.shape[-1]
        o_ref[0, :PAD, :] = jnp.zeros((PAD, d), o_ref.dtype)
        o_ref[0, PAD:, :] = _layer_norm_rows(meta_ref[...], g, b).astype(o_ref.dtype)

    @pl.when(t > 0)
    def _():
        for r0 in range(0, o_ref.shape[1], LN_ROWS):
            rows = slice(r0, r0 + LN_ROWS)
            o_ref[0, rows, :] = _layer_norm_rows(x_ref[0, rows, :], g, b).astype(o_ref.dtype)


def _ln_in(x, meta, g, b):
    nb, seq, d = x.shape
    nt = seq // CHUNK + 1
    return pl.pallas_call(
        _ln_in_kernel,
        grid=(nb, nt),
        in_specs=[
            pl.BlockSpec((1, CHUNK, d), lambda bi, t: (bi, jnp.maximum(t - 1, 0), 0)),
            pl.BlockSpec((N_META, d), lambda bi, t: (0, 0)),
            pl.BlockSpec((1, d), lambda bi, t: (0, 0)),
            pl.BlockSpec((1, d), lambda bi, t: (0, 0)),
        ],
        out_specs=pl.BlockSpec((1, CHUNK, d), lambda bi, t: (bi, t, 0)),
        out_shape=jax.ShapeDtypeStruct((nb, nt * CHUNK, d), BF16),
        name="ln_in",
    )(x, meta, g.reshape(1, d), b.reshape(1, d))


def _cast_weight_chunk(wc_refs, wb_refs, slot, chunk, kc):
    rows = pl.ds(pl.multiple_of(chunk * kc, kc), kc)
    for wc, wb in zip(wc_refs, wb_refs):
        wb[slot, rows, :] = wc[...].astype(BF16)


def _mm_kernel(*refs, n_w, n_e, epilogue, kc):
    x_ref = refs[0]
    wc_refs = refs[1:1 + n_w]
    e_refs = refs[1 + n_w:1 + n_w + n_e]
    o_ref = refs[1 + n_w + n_e]
    wb_refs = refs[2 + n_w + n_e:]
    jj = pl.program_id(0)
    i = pl.program_id(1)
    n_col_tiles = pl.num_programs(0) - 1

    @pl.when(jj < n_col_tiles)
    def _():
        _cast_weight_chunk(wc_refs, wb_refs, lax.rem(jj, 2), i, kc)

    @pl.when(jj > 0)
    def _():
        slot = lax.rem(jj - 1, 2)
        x = x_ref[...]
        accs = [jnp.dot(x, wb[slot], preferred_element_type=F32) for wb in wb_refs]
        o_ref[...] = epilogue(accs, [e[...] for e in e_refs]).astype(o_ref.dtype)


def _matmul(x, ws, extras, epilogue, out_dtype, tm, tn, name):
    r, k = x.shape
    n = ws[0][2]
    nj, ni = n // tn, r // tm
    kc = k // ni
    assert k % ni == 0 and kc % 16 == 0 and n % tn == 0 and r % tm == 0
    assert all(c0 % LANES == 0 for _, c0, _ in ws)
    row = lambda jj, i: jnp.where(jj == 0, 0, i)
    col = lambda jj: jnp.maximum(jj - 1, 0)
    in_specs = [pl.BlockSpec((tm, k), lambda jj, i: (row(jj, i), 0))]
    in_specs += [pl.BlockSpec((pl.Element(kc), pl.Element(tn)),
                              lambda jj, i, c0=c0: (
                                  i * kc, (c0 // LANES + jnp.minimum(jj, nj - 1) * (tn // LANES)) * LANES))
                 for _, c0, _ in ws]
    args = [x] + [w for w, _, _ in ws]
    for arr, off in extras:
        if arr.shape[0] == 1:
            in_specs.append(pl.BlockSpec((1, tn), lambda jj, i, off=off: (0, col(jj) + off)))
        else:
            in_specs.append(pl.BlockSpec((tm, tn), lambda jj, i, off=off: (row(jj, i), col(jj) + off)))
        args.append(arr)
    return pl.pallas_call(
        functools.partial(_mm_kernel, n_w=len(ws), n_e=len(extras), epilogue=epilogue, kc=kc),
        grid=(nj + 1, ni),
        in_specs=in_specs,
        out_specs=pl.BlockSpec((tm, tn), lambda jj, i: (row(jj, i), col(jj))),
        out_shape=jax.ShapeDtypeStruct((r, n), out_dtype),
        scratch_shapes=[pltpu.VMEM((2, k, tn), BF16) for _ in ws],
        compiler_params=pltpu.CompilerParams(
            dimension_semantics=("arbitrary", "arbitrary"), vmem_limit_bytes=VMEM_LIMIT),
        name=name,
    )(*args)


def _mm_conv_kernel(*refs, n_w, taps, halo, tm, tn, n_row_tiles, n_tiles, tiles_per_seq, act, row_group, kc):
    x_ref = refs[0]
    wc_refs = refs[1:1 + n_w]
    cw_ref, cb_ref, o_ref = refs[1 + n_w:4 + n_w]
    wb_refs = refs[4 + n_w:4 + 2 * n_w]
    vbuf, stage = refs[4 + 2 * n_w:]
    step = pl.program_id(0)
    m = step - n_row_tiles
    stride = tm // SUBLANES
    slabs_per_group = MXU_COLS // LANES
    n_groups = tn // MXU_COLS

    @pl.when(step == 0)
    def _():
        vbuf[...] = jnp.zeros(vbuf.shape, F32)

    @pl.when(step < n_tiles)
    def _():
        _cast_weight_chunk(wc_refs, wb_refs, lax.rem(step // n_row_tiles, 2), lax.rem(step, n_row_tiles), kc)

    @pl.when(m >= 0)
    def _():
        cur = jnp.minimum(m, n_tiles - 1)
        slot = lax.rem(cur // n_row_tiles, 2)
        starts_seq = lax.rem(lax.rem(cur, n_row_tiles), tiles_per_seq) == 0

        def column_group(g, carry):
            col0 = pl.multiple_of(g * MXU_COLS, MXU_COLS)
            for s2 in range(slabs_per_group):
                s = g * slabs_per_group + s2
                lanes = pl.ds(pl.multiple_of(col0 + s2 * LANES, LANES), LANES)
                for i0 in range(0, stride, row_group):
                    accs = [cb_ref[:, lanes]] * row_group
                    for k in range(taps):
                        wk = cw_ref[k * SUBLANES:(k + 1) * SUBLANES, lanes]
                        for ii in range(row_group):
                            start = halo + i0 + ii - (taps - 1) + k
                            accs[ii] = accs[ii] + vbuf[s, pl.ds(start, SUBLANES, stride=stride), :] * wk
                    for ii in range(row_group):
                        stage[s, pl.ds(i0 + ii, SUBLANES, stride=stride), :] = act(accs[ii])
                o_ref[:, lanes] = stage[s].astype(o_ref.dtype)

            x = x_ref[...]
            cols = pl.ds(col0, MXU_COLS)
            if n_w == 2:
                gate = _sigmoid(jnp.dot(x, wb_refs[1][slot, :, cols], preferred_element_type=F32))
                for s2 in range(slabs_per_group):
                    stage[g * slabs_per_group + s2] = gate[:, s2 * LANES:(s2 + 1) * LANES]
            v = jnp.dot(x, wb_refs[0][slot, :, cols], preferred_element_type=F32)
            for s2 in range(slabs_per_group):
                s = g * slabs_per_group + s2
                vbuf[s, 0:halo, :] = jnp.where(starts_seq, 0.0, vbuf[s, tm:tm + halo, :])
                v_slab = v[:, s2 * LANES:(s2 + 1) * LANES]
                vbuf[s, halo:halo + tm, :] = v_slab * stage[s] if n_w == 2 else v_slab
            return carry
        lax.fori_loop(0, n_groups, column_group, 0)


def _matmul_conv(x, ws, conv_w, conv_b, act, tm, tn, tiles_per_seq, name):
    r, k = x.shape
    n_cols = ws[0][2]
    taps = conv_w.shape[0]
    halo = -(-(taps - 1) // SUBLANES) * SUBLANES
    n_row_tiles = r // tm
    n_col_tiles = n_cols // tn
    n_tiles = n_row_tiles * n_col_tiles
    kc = k // n_row_tiles
    assert tm % (2 * SUBLANES) == 0 and tn % MXU_COLS == 0 and k % n_row_tiles == 0 and kc % 16 == 0
    assert all(c0 % LANES == 0 for _, c0, _ in ws)
    row_group = 12
    assert (tm // SUBLANES) % row_group == 0
    rep = lambda a: jnp.broadcast_to(a[:, None, :], (a.shape[0], SUBLANES, n_cols)).reshape(-1, n_cols)

    def cur_tile(step):
        c = jnp.clip(step - n_row_tiles, 0, n_tiles - 1)
        return c % n_row_tiles, c // n_row_tiles

    def lag_tile(step):
        c = jnp.clip(step - n_row_tiles - 1, 0, n_tiles - 1)
        return c % n_row_tiles, c // n_row_tiles

    def chunk(step, c0):
        tile_col = jnp.minimum(step // n_row_tiles, n_col_tiles - 1)
        return (step % n_row_tiles) * kc, (c0 // LANES + tile_col * (tn // LANES)) * LANES

    in_specs = [pl.BlockSpec((tm, k), lambda n: (cur_tile(n)[0], 0))]
    in_specs += [pl.BlockSpec((pl.Element(kc), pl.Element(tn)), lambda n, c0=c0: chunk(n, c0))
                 for _, c0, _ in ws]
    in_specs += [pl.BlockSpec((taps * SUBLANES, tn), lambda n: (0, lag_tile(n)[1])),
                 pl.BlockSpec((SUBLANES, tn), lambda n: (0, lag_tile(n)[1]))]
    return pl.pallas_call(
        functools.partial(_mm_conv_kernel, n_w=len(ws), taps=taps, halo=halo, tm=tm, tn=tn,
                          n_row_tiles=n_row_tiles, n_tiles=n_tiles, tiles_per_seq=tiles_per_seq, act=act,
                          row_group=row_group, kc=kc),
        grid=(n_row_tiles + n_tiles + 1,),
        in_specs=in_specs,
        out_specs=pl.BlockSpec((tm, tn), lambda n: lag_tile(n)),
        out_shape=jax.ShapeDtypeStruct((r, n_cols), BF16),
        scratch_shapes=[pltpu.VMEM((2, k, tn), BF16) for _ in ws] + [
            pltpu.VMEM((tn // LANES, halo + tm, LANES), F32),
            pltpu.VMEM((tn // LANES, tm, LANES), F32)],
        compiler_params=pltpu.CompilerParams(
            dimension_semantics=("arbitrary",), vmem_limit_bytes=VMEM_LIMIT),
        name=name,
    )(x, *[w for w, _, _ in ws], rep(conv_w), rep(conv_b.reshape(1, n_cols)))


def _dt_kernel(x_ref, w_ref, bias_ref, o_ref, ot_ref, *, tm, t_rows):
    i = pl.program_id(0)
    raw = jnp.dot(x_ref[...], w_ref[...], preferred_element_type=F32)
    dt = _softplus(raw + bias_ref[...])
    tiles_per_seq = t_rows // tm
    row_in_seq = lax.rem(i, tiles_per_seq) * tm + lax.broadcasted_iota(jnp.int32, dt.shape, 0)
    dt = jnp.where(row_in_seq < PAD, 0.0, dt)
    o_ref[...] = dt
    ot_ref[...] = dt.T


def _dt_proj(h2, w_dt, dt_bias, tm, t_rows):
    r, k = h2.shape
    nh = w_dt.shape[1]
    assert t_rows % tm == 0 and tm % LANES == 0
    return pl.pallas_call(
        functools.partial(_dt_kernel, tm=tm, t_rows=t_rows),
        grid=(r // tm,),
        in_specs=[
            pl.BlockSpec((tm, k), lambda i: (i, 0)),
            pl.BlockSpec((k, nh), lambda i: (0, 0)),
            pl.BlockSpec((1, nh), lambda i: (0, 0)),
        ],
        out_specs=[pl.BlockSpec((tm, nh), lambda i: (i, 0)),
                   pl.BlockSpec((nh, tm), lambda i: (0, i))],
        out_shape=[jax.ShapeDtypeStruct((r, nh), F32), jax.ShapeDtypeStruct((nh, r), F32)],
        compiler_params=pltpu.CompilerParams(
            dimension_semantics=("arbitrary",), vmem_limit_bytes=VMEM_LIMIT),
        name="dt_proj",
    )(h2, w_dt, dt_bias.reshape(1, nh))


def _ln_act_kernel(cv_ref, sg_ref, g_ref, b_ref, o_ref):
    for r0 in range(0, o_ref.shape[0], LN_ROWS):
        rows = slice(r0, r0 + LN_ROWS)
        y = _layer_norm_rows(cv_ref[rows, :].astype(F32), g_ref[...], b_ref[...])
        o_ref[rows, :] = (_silu(y) * sg_ref[rows, :].astype(F32)).astype(o_ref.dtype)


def _ln_act(cv, sg, ln_g, ln_b, tl):
    r, c = cv.shape
    vec = pl.BlockSpec((1, c), lambda i: (0, 0))
    tile = pl.BlockSpec((tl, c), lambda i: (i, 0))
    return pl.pallas_call(
        _ln_act_kernel,
        grid=(r // tl,),
        in_specs=[tile, tile, vec, vec],
        out_specs=tile,
        out_shape=jax.ShapeDtypeStruct((r, c), BF16),
        compiler_params=pltpu.CompilerParams(
            dimension_semantics=("arbitrary",), vmem_limit_bytes=VMEM_LIMIT),
        name="conv_ln_act",
    )(cv, sg, ln_g.reshape(1, c), ln_b.reshape(1, c))


def _ssd_kernel(xbc_ref, dt_ref, dtt_ref, sz_ref, alog_ref, alogt_ref, dskip_ref, ng_ref,
                o_ref, state, *, inner):
    c = pl.program_id(1)
    q = CHUNK
    n = SSD_STATE
    n_groups = state.shape[0]
    gw = inner // n_groups
    pair_w = 2 * SSD_HEAD_DIM
    n_pairs = gw // pair_w
    heads_per_group = gw // SSD_HEAD_DIM

    @pl.when(c == 0)
    def _():
        state[...] = jnp.zeros(state.shape, F32)

    dt = dt_ref[0]
    dtt = dtt_ref[...]
    a_row = -jnp.exp(alog_ref[...])
    a_col = -jnp.exp(alogt_ref[...])

    rows = lax.broadcasted_iota(jnp.int32, (q, q), 0)
    cols = lax.broadcasted_iota(jnp.int32, (q, q), 1)
    causal = cols <= rows
    tri = jnp.where(causal, 1.0, 0.0).astype(F32)
    tri_t = jnp.where(rows <= cols, 1.0, 0.0).astype(F32)
    a_cs = jnp.dot(tri, dt * a_row, preferred_element_type=F32,
                   precision=lax.Precision.HIGHEST)
    a_cs_t = jnp.dot(dtt * a_col, tri_t, preferred_element_type=F32,
                     precision=lax.Precision.HIGHEST)
    cs2 = a_cs * LOG2_E
    cs2_t = a_cs_t * LOG2_E
    src2_t = cs2_t - jnp.log2(dtt)
    w_state_t = dtt * jnp.exp2(cs2_t[:, q - 1:q] - cs2_t)

    lane = lax.broadcasted_iota(jnp.int32, (q, pair_w), 1)
    first = lane < SSD_HEAD_DIM

    for g in range(n_groups):
        bm_b = xbc_ref[0, :, inner + g * n:inner + (g + 1) * n]
        cm_b = xbc_ref[0, :, inner + (n_groups + g) * n:inner + (n_groups + g + 1) * n]
        cb = lax.dot_general(cm_b, bm_b, (((1,), (1,)), ((), ())), preferred_element_type=F32)
        bm_t = bm_b.astype(F32).T

        y_parts = []
        for j in range(n_pairs):
            h0 = g * heads_per_group + 2 * j
            h1 = h0 + 1
            lo, hi = g * gw + j * pair_w, g * gw + (j + 1) * pair_w
            ac0 = jnp.broadcast_to(cs2[:, h0:h0 + 1], (q, q))
            ac1 = jnp.broadcast_to(cs2[:, h1:h1 + 1], (q, q))
            ac_pair = jnp.where(first, ac0, ac1)
            tot_pair = ac_pair[q - 1:q, :]

            x_b = xbc_ref[0, :, lo:hi]
            zero = jnp.zeros_like(x_b)
            x_bd = jnp.concatenate([jnp.where(first, x_b, zero), jnp.where(first, zero, x_b)], axis=0)

            seg0 = jnp.where(causal, ac0 - src2_t[h0:h0 + 1, :], -jnp.inf)
            seg1 = jnp.where(causal, ac1 - src2_t[h1:h1 + 1, :], -jnp.inf)
            m0 = (cb * jnp.exp2(seg0)).astype(BF16)
            m1 = (cb * jnp.exp2(seg1)).astype(BF16)
            y_diag = jnp.dot(jnp.concatenate([m0, m1], axis=1), x_bd, preferred_element_type=F32)

            st = state[g, :, j * pair_w:(j + 1) * pair_w]
            y_off = jnp.dot(cm_b, st.astype(BF16), preferred_element_type=F32) * jnp.exp2(ac_pair)
            y_parts.append(y_diag + y_off + dskip_ref[:, lo:hi] * x_b.astype(F32))

            w0 = (bm_t * w_state_t[h0:h0 + 1, :]).astype(BF16)
            w1 = (bm_t * w_state_t[h1:h1 + 1, :]).astype(BF16)
            upd = jnp.dot(jnp.concatenate([w0, w1], axis=1), x_bd, preferred_element_type=F32)
            state[g, :, j * pair_w:(j + 1) * pair_w] = st * jnp.exp2(tot_pair) + upd

        y = jnp.concatenate(y_parts, axis=1)
        yz = y * sz_ref[0, :, g * gw:(g + 1) * gw].astype(F32)
        ms = jnp.mean(yz * yz, axis=-1, keepdims=True)
        o_ref[0, :, g * gw:(g + 1) * gw] = (
            yz * lax.rsqrt(ms + RMS_EPS) * ng_ref[:, g * gw:(g + 1) * gw]).astype(o_ref.dtype)


def _ssd_branch(xbc3, dt3, dt_t, sz3, a_log, d_skip, norm_g):
    nb, t_rows, xbc_w = xbc3.shape
    inner = sz3.shape[-1]
    heads = a_log.shape[0]
    nc = t_rows // CHUNK
    dskip_x = jnp.repeat(d_skip, SSD_HEAD_DIM).reshape(1, inner)
    full = lambda shape: pl.BlockSpec(shape, lambda bi, ci: (0,) * len(shape))

    return pl.pallas_call(
        functools.partial(_ssd_kernel, inner=inner),
        grid=(nb, nc),
        in_specs=[
            pl.BlockSpec((1, CHUNK, xbc_w), lambda bi, ci: (bi, ci, 0)),
            pl.BlockSpec((1, CHUNK, heads), lambda bi, ci: (bi, ci, 0)),
            pl.BlockSpec((heads, CHUNK), lambda bi, ci: (0, bi * nc + ci)),
            pl.BlockSpec((1, CHUNK, inner), lambda bi, ci: (bi, ci, 0)),
            full((1, heads)), full((heads, 1)), full((1, inner)), full((1, inner)),
        ],
        out_specs=pl.BlockSpec((1, CHUNK, inner), lambda bi, ci: (bi, ci, 0)),
        out_shape=jax.ShapeDtypeStruct((nb, t_rows, inner), BF16),
        scratch_shapes=[pltpu.VMEM((SSD_GROUPS, SSD_STATE, inner // SSD_GROUPS), F32)],
        compiler_params=pltpu.CompilerParams(
            dimension_semantics=("arbitrary", "arbitrary"), vmem_limit_bytes=VMEM_LIMIT),
        name="ssd_branch",
    )(xbc3, dt3, dt_t, sz3, a_log.reshape(1, heads), a_log.reshape(heads, 1), dskip_x,
      norm_g.reshape(1, inner))


def _ln_out_kernel(x_ref, sub_ref, gi_ref, bi_ref, go_ref, bo_ref, o_ref):
    h = _layer_norm_rows(x_ref[0], gi_ref[...], bi_ref[...])
    o_ref[0] = _layer_norm_rows(DEEPNORM_ALPHA * h + sub_ref[0].astype(F32), go_ref[...], bo_ref[...])


def _ln_out(x, sub3, gi, bi, go, bo):
    nb, seq, d = x.shape
    row = lambda a: a.reshape(1, d)
    vec = pl.BlockSpec((1, d), lambda b_, t: (0, 0))
    return pl.pallas_call(
        _ln_out_kernel,
        grid=(nb, seq // CHUNK),
        in_specs=[
            pl.BlockSpec((1, CHUNK, d), lambda b_, t: (b_, t, 0)),
            pl.BlockSpec((1, CHUNK, d), lambda b_, t: (b_, t + 1, 0)),
            vec, vec, vec, vec,
        ],
        out_specs=pl.BlockSpec((1, CHUNK, d), lambda b_, t: (b_, t, 0)),
        out_shape=jax.ShapeDtypeStruct((nb, seq, d), F32),
        name="ln_out",
    )(x, sub3, row(gi), row(bi), row(go), row(bo))


def kernel(x, meta_tokens, ln_in_g, ln_in_b, w_in, b_gate, conv_w, conv_b, conv_ln_g, conv_ln_b,
           w_conv_out, ssd_conv_w, ssd_conv_b, dt_bias, a_log, d_skip, ssd_norm_g, w_ssd_out,
           w_out, ln_out_g, ln_out_b):
    nb, seq, d = x.shape
    assert w_in.shape[0] == 1, "single-layer trunk"
    assert seq % CHUNK == 0
    t_rows = seq + CHUNK
    r = nb * t_rows
    inner = w_ssd_out.shape[1]
    heads = a_log.shape[1]
    xbc_w = ssd_conv_w.shape[2]
    tiles_per_seq = 4
    tm = t_rows // tiles_per_seq
    tn = 1024
    assert t_rows % tiles_per_seq == 0 and tm % 16 == 0

    o_val, o_glu, o_gate, o_z = 0, d, 2 * d, 3 * d
    o_xbc = o_z + inner
    o_dt = o_xbc + xbc_w
    o_g = o_dt + heads
    w = w_in[0]

    h3 = _ln_in(x, meta_tokens, ln_in_g, ln_in_b)
    h2 = h3.reshape(r, d)

    cv = _matmul_conv(h2, [(w, o_val, d), (w, o_glu, d)], conv_w[0], conv_b[0],
                      lambda a: a, tm, tn // 2, tiles_per_seq, "in_proj_glu_conv")
    sg = _matmul(h2, [(w, o_gate, d)], [], lambda a, e: _silu(a[0]), BF16, tm, tn, "in_proj_gate")
    sz = _matmul(h2, [(w, o_z, inner)], [], lambda a, e: _silu(a[0]), BF16, tm, tn, "in_proj_z")
    xbc = _matmul_conv(h2, [(w, o_xbc, xbc_w)], ssd_conv_w[0], ssd_conv_b[0],
                       _silu, tm, tn, tiles_per_seq, "in_proj_xbc_conv")
    dt, dt_t = _dt_proj(h2, w[:, o_dt:o_g].astype(BF16), dt_bias[0], t_rows // 3, t_rows)
    gates = _matmul(h2, [(w, o_g, 2 * d)], [(b_gate[0].reshape(1, 2 * d), 0)],
                    lambda a, e: _sigmoid(a[0] + e[0]), BF16, tm, tn, "in_proj_merge_gates")

    u = _ln_act(cv, sg, conv_ln_g[0], conv_ln_b[0], tm // 6)
    yn = _ssd_branch(xbc.reshape(nb, t_rows, xbc_w), dt.reshape(nb, t_rows, heads), dt_t,
                     sz.reshape(nb, t_rows, inner), a_log[0], d_skip[0], ssd_norm_g[0])

    gated_conv = _matmul(u, [(w_conv_out[0], 0, d)], [(gates, 0)],
                         lambda a, e: a[0] * e[0].astype(F32), BF16, tm, tn, "conv_out_proj")
    merged = _matmul(yn.reshape(r, inner), [(w_ssd_out[0], 0, d)],
                     [(gates, 2 * d // tn), (gated_conv, 0)],
                     lambda a, e: a[0] * e[0].astype(F32) + e[1].astype(F32),
                     BF16, tm // 2, tn // 2, "ssd_out_proj_merge")
    sub = _matmul(merged, [(w_out[0], 0, d)], [], lambda a, e: a[0], F32, tm, tn, "out_proj")

    return _ln_out(x, sub.reshape(nb, t_rows, d), ln_in_g, ln_in_b, ln_out_g[0], ln_out_b[0])
```

```python
import functools

import jax
import jax.numpy as jnp
from jax import lax
from jax.experimental import pallas as pl
from jax.experimental.pallas import tpu as pltpu

F32 = jnp.float32
BF16 = jnp.bfloat16

N_META = 16
CHUNK = 128
PAD = CHUNK - N_META
CONV_WIDTH = 31
SSD_HEAD_DIM = 64
SSD_GROUPS = 8
SSD_STATE = 128
SSD_CONV_WIDTH = 4
LN_EPS = 1e-5
RMS_EPS = 1e-5
DEEPNORM_ALPHA = 2.0 ** 0.25
LOG2_E = 1.4426950408889634

SUBLANES = 8
LANES = 128
MXU_COLS = 256
LN_ROWS = 16
VMEM_LIMIT = 56 * 1024 * 1024


def _sigmoid(x):
    return 0.5 * jnp.tanh(0.5 * x) + 0.5


def _silu_of_half(h):
    return h + h * jnp.tanh(h)


def _silu(x):
    return _silu_of_half(0.5 * x)


def _softplus(x):
    return jnp.maximum(x, 0.0) + jnp.log1p(jnp.exp(-jnp.abs(x)))


def _layer_norm_rows(v, g, b):
    mu = jnp.mean(v, axis=-1, keepdims=True)
    d = v - mu
    var = jnp.mean(d * d, axis=-1, keepdims=True)
    return d * lax.rsqrt(var + LN_EPS) * g + b


def _ln_in_kernel(x_ref, meta_ref, g_ref, b_ref, o_ref):
    t = pl.program_id(1)
    g = g_ref[...]
    b = b_ref[...]

    @pl.when(t == 0)
    def _():
        d = o_ref.shape[-1]
        o_ref[0, :PAD, :] = jnp.zeros((PAD, d), o_ref.dtype)
        o_ref[0, PAD:, :] = _layer_norm_rows(meta_ref[...], g, b).astype(o_ref.dtype)

    @pl.when(t > 0)
    def _():
        for r0 in range(0, o_ref.shape[1], LN_ROWS):
            rows = slice(r0, r0 + LN_ROWS)
            o_ref[0, rows, :] = _layer_norm_rows(x_ref[0, rows, :], g, b).astype(o_ref.dtype)


def _ln_in(x, meta, g, b):
    nb, seq, d = x.shape
    nt = seq // CHUNK + 1
    return pl.pallas_call(
        _ln_in_kernel,
        grid=(nb, nt),
        in_specs=[
            pl.BlockSpec((1, CHUNK, d), lambda bi, t: (bi, jnp.maximum(t - 1, 0), 0)),
            pl.BlockSpec((N_META, d), lambda bi, t: (0, 0)),
            pl.BlockSpec((1, d), lambda bi, t: (0, 0)),
            pl.BlockSpec((1, d), lambda bi, t: (0, 0)),
        ],
        out_specs=pl.BlockSpec((1, CHUNK, d), lambda bi, t: (bi, t, 0)),
        out_shape=jax.ShapeDtypeStruct((nb, nt * CHUNK, d), BF16),
        name="ln_in",
    )(x, meta, g.reshape(1, d), b.reshape(1, d))


def _cast_weight_chunk(wc_refs, wb_refs, slot, chunk, kc):
    rows = pl.ds(pl.multiple_of(chunk * kc, kc), kc)
    for wc, wb in zip(wc_refs, wb_refs):
        wb[slot, rows, :] = wc[...].astype(BF16)


def _mm_kernel(*refs, n_w, n_e, epilogue, kc):
    x_ref = refs[0]
    wc_refs = refs[1:1 + n_w]
    e_refs = refs[1 + n_w:1 + n_w + n_e]
    o_ref = refs[1 + n_w + n_e]
    wb_refs = refs[2 + n_w + n_e:]
    jj = pl.program_id(0)
    i = pl.program_id(1)
    n_col_tiles = pl.num_programs(0) - 1

    @pl.when(jj < n_col_tiles)
    def _():
        _cast_weight_chunk(wc_refs, wb_refs, lax.rem(jj, 2), i, kc)

    @pl.when(jj > 0)
    def _():
        slot = lax.rem(jj - 1, 2)
        x = x_ref[...]
        accs = [jnp.dot(x, wb[slot], preferred_element_type=F32) for wb in wb_refs]
        o_ref[...] = epilogue(accs, [e[...] for e in e_refs]).astype(o_ref.dtype)


def _matmul(x, ws, extras, epilogue, out_dtype, tm, tn, name):
    r, k = x.shape
    n = ws[0][2]
    nj, ni = n // tn, r // tm
    kc = k // ni
    assert k % ni == 0 and kc % 16 == 0 and n % tn == 0 and r % tm == 0
    assert all(c0 % LANES == 0 for _, c0, _ in ws)
    row = lambda jj, i: jnp.where(jj == 0, 0, i)
    col = lambda jj: jnp.maximum(jj - 1, 0)
    in_specs = [pl.BlockSpec((tm, k), lambda jj, i: (row(jj, i), 0))]
    in_specs += [pl.BlockSpec((pl.Element(kc), pl.Element(tn)),
                              lambda jj, i, c0=c0: (
                                  i * kc, (c0 // LANES + jnp.minimum(jj, nj - 1) * (tn // LANES)) * LANES))
                 for _, c0, _ in ws]
    args = [x] + [w for w, _, _ in ws]
    for arr, off in extras:
        if arr.shape[0] == 1:
            in_specs.append(pl.BlockSpec((1, tn), lambda jj, i, off=off: (0, col(jj) + off)))
        else:
            in_specs.append(pl.BlockSpec((tm, tn), lambda jj, i, off=off: (row(jj, i), col(jj) + off)))
        args.append(arr)
    return pl.pallas_call(
        functools.partial(_mm_kernel, n_w=len(ws), n_e=len(extras), epilogue=epilogue, kc=kc),
        grid=(nj + 1, ni),
        in_specs=in_specs,
        out_specs=pl.BlockSpec((tm, tn), lambda jj, i: (row(jj, i), col(jj))),
        out_shape=jax.ShapeDtypeStruct((r, n), out_dtype),
        scratch_shapes=[pltpu.VMEM((2, k, tn), BF16) for _ in ws],
        compiler_params=pltpu.CompilerParams(
            dimension_semantics=("arbitrary", "arbitrary"), vmem_limit_bytes=VMEM_LIMIT),
        name=name,
    )(*args)


def _mm_conv_kernel(*refs, n_w, taps, halo, tm, tn, n_row_tiles, n_tiles, tiles_per_seq, act, row_group, kc):
    x_ref = refs[0]
    wc_refs = refs[1:1 + n_w]
    cw_ref, cb_ref, o_ref = refs[1 + n_w:4 + n_w]
    wb_refs = refs[4 + n_w:4 + 2 * n_w]
    vbuf, stage = refs[4 + 2 * n_w:]
    step = pl.program_id(0)
    m = step - n_row_tiles
    stride = tm // SUBLANES
    slabs_per_group = MXU_COLS // LANES
    n_groups = tn // MXU_COLS

    @pl.when(step == 0)
    def _():
        vbuf[...] = jnp.zeros(vbuf.shape, F32)

    @pl.when(step < n_tiles)
    def _():
        _cast_weight_chunk(wc_refs, wb_refs, lax.rem(step // n_row_tiles, 2), lax.rem(step, n_row_tiles), kc)

    @pl.when(m >= 0)
    def _():
        cur = jnp.minimum(m, n_tiles - 1)
        slot = lax.rem(cur // n_row_tiles, 2)
        starts_seq = lax.rem(lax.rem(cur, n_row_tiles), tiles_per_seq) == 0

        def column_group(g, carry):
            col0 = pl.multiple_of(g * MXU_COLS, MXU_COLS)
            for s2 in range(slabs_per_group):
                s = g * slabs_per_group + s2
                lanes = pl.ds(pl.multiple_of(col0 + s2 * LANES, LANES), LANES)
                for i0 in range(0, stride, row_group):
                    accs = [cb_ref[:, lanes]] * row_group
                    for k in range(taps):
                        wk = cw_ref[k * SUBLANES:(k + 1) * SUBLANES, lanes]
                        for ii in range(row_group):
                            start = halo + i0 + ii - (taps - 1) + k
                            accs[ii] = accs[ii] + vbuf[s, pl.ds(start, SUBLANES, stride=stride), :] * wk
                    for ii in range(row_group):
                        stage[s, pl.ds(i0 + ii, SUBLANES, stride=stride), :] = act(accs[ii])
                o_ref[:, lanes] = stage[s].astype(o_ref.dtype)

            x = x_ref[...]
            cols = pl.ds(col0, MXU_COLS)
            if n_w == 2:
                gate = _sigmoid(jnp.dot(x, wb_refs[1][slot, :, cols], preferred_element_type=F32))
                for s2 in range(slabs_per_group):
                    stage[g * slabs_per_group + s2] = gate[:, s2 * LANES:(s2 + 1) * LANES]
            v = jnp.dot(x, wb_refs[0][slot, :, cols], preferred_element_type=F32)
            for s2 in range(slabs_per_group):
                s = g * slabs_per_group + s2
                vbuf[s, 0:halo, :] = jnp.where(starts_seq, 0.0, vbuf[s, tm:tm + halo, :])
                v_slab = v[:, s2 * LANES:(s2 + 1) * LANES]
                vbuf[s, halo:halo + tm, :] = v_slab * stage[s] if n_w == 2 else v_slab
            return carry
        lax.fori_loop(0, n_groups, column_group, 0)


def _matmul_conv(x, ws, conv_w, conv_b, act, tm, tn, tiles_per_seq, name):
    r, k = x.shape
    n_cols = ws[0][2]
    taps = conv_w.shape[0]
    halo = -(-(taps - 1) // SUBLANES) * SUBLANES
    n_row_tiles = r // tm
    n_col_tiles = n_cols // tn
    n_tiles = n_row_tiles * n_col_tiles
    kc = k // n_row_tiles
    assert tm % (2 * SUBLANES) == 0 and tn % MXU_COLS == 0 and k % n_row_tiles == 0 and kc % 16 == 0
    assert all(c0 % LANES == 0 for _, c0, _ in ws)
    row_group = 12
    assert (tm // SUBLANES) % row_group == 0
    rep = lambda a: jnp.broadcast_to(a[:, None, :], (a.shape[0], SUBLANES, n_cols)).reshape(-1, n_cols)

    def cur_tile(step):
        c = jnp.clip(step - n_row_tiles, 0, n_tiles - 1)
        return c % n_row_tiles, c // n_row_tiles

    def lag_tile(step):
        c = jnp.clip(step - n_row_tiles - 1, 0, n_tiles - 1)
        return c % n_row_tiles, c // n_row_tiles

    def chunk(step, c0):
        tile_col = jnp.minimum(step // n_row_tiles, n_col_tiles - 1)
        return (step % n_row_tiles) * kc, (c0 // LANES + tile_col * (tn // LANES)) * LANES

    in_specs = [pl.BlockSpec((tm, k), lambda n: (cur_tile(n)[0], 0))]
    in_specs += [pl.BlockSpec((pl.Element(kc), pl.Element(tn)), lambda n, c0=c0: chunk(n, c0))
                 for _, c0, _ in ws]
    in_specs += [pl.BlockSpec((taps * SUBLANES, tn), lambda n: (0, lag_tile(n)[1])),
                 pl.BlockSpec((SUBLANES, tn), lambda n: (0, lag_tile(n)[1]))]
    return pl.pallas_call(
        functools.partial(_mm_conv_kernel, n_w=len(ws), taps=taps, halo=halo, tm=tm, tn=tn,
                          n_row_tiles=n_row_tiles, n_tiles=n_tiles, tiles_per_seq=tiles_per_seq, act=act,
                          row_group=row_group, kc=kc),
        grid=(n_row_tiles + n_tiles + 1,),
        in_specs=in_specs,
        out_specs=pl.BlockSpec((tm, tn), lambda n: lag_tile(n)),
        out_shape=jax.ShapeDtypeStruct((r, n_cols), BF16),
        scratch_shapes=[pltpu.VMEM((2, k, tn), BF16) for _ in ws] + [
            pltpu.VMEM((tn // LANES, halo + tm, LANES), F32),
            pltpu.VMEM((tn // LANES, tm, LANES), F32)],
        compiler_params=pltpu.CompilerParams(
            dimension_semantics=("arbitrary",), vmem_limit_bytes=VMEM_LIMIT),
        name=name,
    )(x, *[w for w, _, _ in ws], rep(conv_w), rep(conv_b.reshape(1, n_cols)))


def _dt_kernel(x_ref, w_ref, bias_ref, o_ref, ot_ref, *, tm, t_rows):
    i = pl.program_id(0)
    raw = jnp.dot(x_ref[...], w_ref[...], preferred_element_type=F32)
    dt = _softplus(raw + bias_ref[...])
    tiles_per_seq = t_rows // tm
    row_in_seq = lax.rem(i, tiles_per_seq) * tm + lax.broadcasted_iota(jnp.int32, dt.shape, 0)
    dt = jnp.where(row_in_seq < PAD, 0.0, dt)
    o_ref[...] = dt
    ot_ref[...] = dt.T


def _dt_proj(h2, w_dt, dt_bias, tm, t_rows):
    r, k = h2.shape
    nh = w_dt.shape[1]
    assert t_rows % tm == 0 and tm % LANES == 0
    return pl.pallas_call(
        functools.partial(_dt_kernel, tm=tm, t_rows=t_rows),
        grid=(r // tm,),
        in_specs=[
            pl.BlockSpec((tm, k), lambda i: (i, 0)),
            pl.BlockSpec((k, nh), lambda i: (0, 0)),
            pl.BlockSpec((1, nh), lambda i: (0, 0)),
        ],
        out_specs=[pl.BlockSpec((tm, nh), lambda i: (i, 0)),
                   pl.BlockSpec((nh, tm), lambda i: (0, i))],
        out_shape=[jax.ShapeDtypeStruct((r, nh), F32), jax.ShapeDtypeStruct((nh, r), F32)],
        compiler_params=pltpu.CompilerParams(
            dimension_semantics=("arbitrary",), vmem_limit_bytes=VMEM_LIMIT),
        name="dt_proj",
    )(h2, w_dt, dt_bias.reshape(1, nh))


def _ln_act_kernel(cv_ref, sg_ref, g_ref, b_ref, o_ref):
    half_g = 0.5 * g_ref[...]
    half_b = 0.5 * b_ref[...]
    for r0 in range(0, o_ref.shape[0], LN_ROWS):
        rows = slice(r0, r0 + LN_ROWS)
        half_y = _layer_norm_rows(cv_ref[rows, :].astype(F32), half_g, half_b)
        o_ref[rows, :] = (_silu_of_half(half_y) * sg_ref[rows, :].astype(F32)).astype(o_ref.dtype)


def _ln_act(cv, sg, ln_g, ln_b, tl):
    r, c = cv.shape
    vec = pl.BlockSpec((1, c), lambda i: (0, 0))
    tile = pl.BlockSpec((tl, c), lambda i: (i, 0))
    return pl.pallas_call(
        _ln_act_kernel,
        grid=(r // tl,),
        in_specs=[tile, tile, vec, vec],
        out_specs=tile,
        out_shape=jax.ShapeDtypeStruct((r, c), BF16),
        compiler_params=pltpu.CompilerParams(
            dimension_semantics=("arbitrary",), vmem_limit_bytes=VMEM_LIMIT),
        name="conv_ln_act",
    )(cv, sg, ln_g.reshape(1, c), ln_b.reshape(1, c))


def _ssd_kernel(xbc_ref, dt_ref, dtt_ref, sz_ref, alog_ref, alogt_ref, dskip_ref, ng_ref,
                o_ref, state, *, inner):
    c = pl.program_id(1)
    q = CHUNK
    n = SSD_STATE
    n_groups = state.shape[0]
    gw = inner // n_groups
    pair_w = 2 * SSD_HEAD_DIM
    n_pairs = gw // pair_w
    heads_per_group = gw // SSD_HEAD_DIM

    @pl.when(c == 0)
    def _():
        state[...] = jnp.zeros(state.shape, F32)

    dt = dt_ref[0]
    dtt = dtt_ref[...]
    a_row = -jnp.exp(alog_ref[...])
    a_col = -jnp.exp(alogt_ref[...])

    rows = lax.broadcasted_iota(jnp.int32, (q, q), 0)
    cols = lax.broadcasted_iota(jnp.int32, (q, q), 1)
    causal = cols <= rows
    tri = jnp.where(causal, 1.0, 0.0).astype(F32)
    tri_t = jnp.where(rows <= cols, 1.0, 0.0).astype(F32)
    a_cs = jnp.dot(tri, dt * a_row, preferred_element_type=F32,
                   precision=lax.Precision.HIGHEST)
    a_cs_t = jnp.dot(dtt * a_col, tri_t, preferred_element_type=F32,
                     precision=lax.Precision.HIGHEST)
    cs2 = a_cs * LOG2_E
    cs2_t = a_cs_t * LOG2_E
    src2_t = cs2_t - jnp.log2(dtt)
    w_state_t = dtt * jnp.exp2(cs2_t[:, q - 1:q] - cs2_t)

    lane = lax.broadcasted_iota(jnp.int32, (q, pair_w), 1)
    first = lane < SSD_HEAD_DIM

    for g in range(n_groups):
        bm_b = xbc_ref[0, :, inner + g * n:inner + (g + 1) * n]
        cm_b = xbc_ref[0, :, inner + (n_groups + g) * n:inner + (n_groups + g + 1) * n]
        cb = lax.dot_general(cm_b, bm_b, (((1,), (1,)), ((), ())), preferred_element_type=F32)
        bm_t = bm_b.astype(F32).T

        y_parts = []
        for j in range(n_pairs):
            h0 = g * heads_per_group + 2 * j
            h1 = h0 + 1
            lo, hi = g * gw + j * pair_w, g * gw + (j + 1) * pair_w
            ac0 = jnp.broadcast_to(cs2[:, h0:h0 + 1], (q, q))
            ac1 = jnp.broadcast_to(cs2[:, h1:h1 + 1], (q, q))
            ac_pair = jnp.where(first, ac0, ac1)
            tot_pair = ac_pair[q - 1:q, :]

            x_b = xbc_ref[0, :, lo:hi]
            zero = jnp.zeros_like(x_b)
            x_bd = jnp.concatenate([jnp.where(first, x_b, zero), jnp.where(first, zero, x_b)], axis=0)

            seg0 = jnp.where(causal, ac0 - src2_t[h0:h0 + 1, :], -jnp.inf)
            seg1 = jnp.where(causal, ac1 - src2_t[h1:h1 + 1, :], -jnp.inf)
            m0 = (cb * jnp.exp2(seg0)).astype(BF16)
            m1 = (cb * jnp.exp2(seg1)).astype(BF16)
            y_diag = jnp.dot(jnp.concatenate([m0, m1], axis=1), x_bd, preferred_element_type=F32)

            st = state[g, :, j * pair_w:(j + 1) * pair_w]
            y_off = jnp.dot(cm_b, st.astype(BF16), preferred_element_type=F32) * jnp.exp2(ac_pair)
            y_parts.append(y_diag + y_off + dskip_ref[:, lo:hi] * x_b.astype(F32))

            w0 = (bm_t * w_state_t[h0:h0 + 1, :]).astype(BF16)
            w1 = (bm_t * w_state_t[h1:h1 + 1, :]).astype(BF16)
            upd = jnp.dot(jnp.concatenate([w0, w1], axis=1), x_bd, preferred_element_type=F32)
            state[g, :, j * pair_w:(j + 1) * pair_w] = st * jnp.exp2(tot_pair) + upd

        y = jnp.concatenate(y_parts, axis=1)
        yz = y * sz_ref[0, :, g * gw:(g + 1) * gw].astype(F32)
        ms = jnp.mean(yz * yz, axis=-1, keepdims=True)
        o_ref[0, :, g * gw:(g + 1) * gw] = (
            yz * lax.rsqrt(ms + RMS_EPS) * ng_ref[:, g * gw:(g + 1) * gw]).astype(o_ref.dtype)


def _ssd_branch(xbc3, dt3, dt_t, sz3, a_log, d_skip, norm_g):
    nb, t_rows, xbc_w = xbc3.shape
    inner = sz3.shape[-1]
    heads = a_log.shape[0]
    nc = t_rows // CHUNK
    dskip_x = jnp.repeat(d_skip, SSD_HEAD_DIM).reshape(1, inner)
    full = lambda shape: pl.BlockSpec(shape, lambda bi, ci: (0,) * len(shape))

    return pl.pallas_call(
        functools.partial(_ssd_kernel, inner=inner),
        grid=(nb, nc),
        in_specs=[
            pl.BlockSpec((1, CHUNK, xbc_w), lambda bi, ci: (bi, ci, 0)),
            pl.BlockSpec((1, CHUNK, heads), lambda bi, ci: (bi, ci, 0)),
            pl.BlockSpec((heads, CHUNK), lambda bi, ci: (0, bi * nc + ci)),
            pl.BlockSpec((1, CHUNK, inner), lambda bi, ci: (bi, ci, 0)),
            full((1, heads)), full((heads, 1)), full((1, inner)), full((1, inner)),
        ],
        out_specs=pl.BlockSpec((1, CHUNK, inner), lambda bi, ci: (bi, ci, 0)),
        out_shape=jax.ShapeDtypeStruct((nb, t_rows, inner), BF16),
        scratch_shapes=[pltpu.VMEM((SSD_GROUPS, SSD_STATE, inner // SSD_GROUPS), F32)],
        compiler_params=pltpu.CompilerParams(
            dimension_semantics=("arbitrary", "arbitrary"), vmem_limit_bytes=VMEM_LIMIT),
        name="ssd_branch",
    )(xbc3, dt3, dt_t, sz3, a_log.reshape(1, heads), a_log.reshape(heads, 1), dskip_x,
      norm_g.reshape(1, inner))


def _ln_out_kernel(x_ref, sub_ref, gi_ref, bi_ref, go_ref, bo_ref, o_ref):
    h = _layer_norm_rows(x_ref[0], gi_ref[...], bi_ref[...])
    o_ref[0] = _layer_norm_rows(DEEPNORM_ALPHA * h + sub_ref[0].astype(F32), go_ref[...], bo_ref[...])


def _ln_out(x, sub3, gi, bi, go, bo):
    nb, seq, d = x.shape
    row = lambda a: a.reshape(1, d)
    vec = pl.BlockSpec((1, d), lambda b_, t: (0, 0))
    return pl.pallas_call(
        _ln_out_kernel,
        grid=(nb, seq // CHUNK),
        in_specs=[
            pl.BlockSpec((1, CHUNK, d), lambda b_, t: (b_, t, 0)),
            pl.BlockSpec((1, CHUNK, d), lambda b_, t: (b_, t + 1, 0)),
            vec, vec, vec, vec,
        ],
        out_specs=pl.BlockSpec((1, CHUNK, d), lambda b_, t: (b_, t, 0)),
        out_shape=jax.ShapeDtypeStruct((nb, seq, d), F32),
        name="ln_out",
    )(x, sub3, row(gi), row(bi), row(go), row(bo))


def kernel(x, meta_tokens, ln_in_g, ln_in_b, w_in, b_gate, conv_w, conv_b, conv_ln_g, conv_ln_b,
           w_conv_out, ssd_conv_w, ssd_conv_b, dt_bias, a_log, d_skip, ssd_norm_g, w_ssd_out,
           w_out, ln_out_g, ln_out_b):
    nb, seq, d = x.shape
    assert w_in.shape[0] == 1, "single-layer trunk"
    assert seq % CHUNK == 0
    t_rows = seq + CHUNK
    r = nb * t_rows
    inner = w_ssd_out.shape[1]
    heads = a_log.shape[1]
    xbc_w = ssd_conv_w.shape[2]
    tiles_per_seq = 4
    tm = t_rows // tiles_per_seq
    tn = 1024
    assert t_rows % tiles_per_seq == 0 and tm % 16 == 0

    o_val, o_glu, o_gate, o_z = 0, d, 2 * d, 3 * d
    o_xbc = o_z + inner
    o_dt = o_xbc + xbc_w
    o_g = o_dt + heads
    w = w_in[0]

    h3 = _ln_in(x, meta_tokens, ln_in_g, ln_in_b)
    h2 = h3.reshape(r, d)

    cv = _matmul_conv(h2, [(w, o_val, d), (w, o_glu, d)], conv_w[0], conv_b[0],
                      lambda a: a, tm, tn // 2, tiles_per_seq, "in_proj_glu_conv")
    sg = _matmul(h2, [(w, o_gate, d)], [], lambda a, e: _silu(a[0]), BF16, tm, tn, "in_proj_gate")
    sz = _matmul(h2, [(w, o_z, inner)], [], lambda a, e: _silu(a[0]), BF16, tm, tn, "in_proj_z")
    xbc = _matmul_conv(h2, [(w, o_xbc, xbc_w)], 0.5 * ssd_conv_w[0], 0.5 * ssd_conv_b[0],
                       _silu_of_half, tm, tn, tiles_per_seq, "in_proj_xbc_conv")
    dt, dt_t = _dt_proj(h2, w[:, o_dt:o_g].astype(BF16), dt_bias[0], t_rows // 3, t_rows)
    gates = _matmul(h2, [(w, o_g, 2 * d)], [(b_gate[0].reshape(1, 2 * d), 0)],
                    lambda a, e: _sigmoid(a[0] + e[0]), BF16, tm, tn, "in_proj_merge_gates")

    u = _ln_act(cv, sg, conv_ln_g[0], conv_ln_b[0], tm // 6)
    yn = _ssd_branch(xbc.reshape(nb, t_rows, xbc_w), dt.reshape(nb, t_rows, heads), dt_t,
                     sz.reshape(nb, t_rows, inner), a_log[0], d_skip[0], ssd_norm_g[0])

    gated_conv = _matmul(u, [(w_conv_out[0], 0, d)], [(gates, 0)],
                         lambda a, e: a[0] * e[0].astype(F32), BF16, tm, tn, "conv_out_proj")
    merged = _matmul(yn.reshape(r, inner), [(w_ssd_out[0], 0, d)],
                     [(gates, 2 * d // tn), (gated_conv, 0)],
                     lambda a, e: a[0] * e[0].astype(F32) + e[1].astype(F32),
                     BF16, tm // 2, tn // 2, "ssd_out_proj_merge")
    sub = _matmul(merged, [(w_out[0], 0, d)], [], lambda a, e: a[0], F32, tm, tn, "out_proj")

    return _ln_out(x, sub.reshape(nb, t_rows, d), ln_in_g, ln_in_b, ln_out_g[0], ln_out_b[0])
```

```python
import functools

import jax
import jax.numpy as jnp
from jax import lax
from jax.experimental import pallas as pl
from jax.experimental.pallas import tpu as pltpu

F32 = jnp.float32
BF16 = jnp.bfloat16

N_META = 16
CHUNK = 128
PAD = CHUNK - N_META
CONV_WIDTH = 31
SSD_HEAD_DIM = 64
SSD_GROUPS = 8
SSD_STATE = 128
SSD_CONV_WIDTH = 4
LN_EPS = 1e-5
RMS_EPS = 1e-5
DEEPNORM_ALPHA = 2.0 ** 0.25
LOG2_E = 1.4426950408889634

SUBLANES = 8
LANES = 128
MXU_COLS = 256
LN_ROWS = 16
VMEM_LIMIT = 56 * 1024 * 1024


def _sigmoid(x):
    return 0.5 * jnp.tanh(0.5 * x) + 0.5


def _silu_of_half(h):
    return h + h * jnp.tanh(h)


def _silu(x):
    return _silu_of_half(0.5 * x)


def _softplus(x):
    return jnp.maximum(x, 0.0) + jnp.log1p(jnp.exp(-jnp.abs(x)))


def _layer_norm_rows(v, g, b):
    mu = jnp.mean(v, axis=-1, keepdims=True)
    d = v - mu
    var = jnp.mean(d * d, axis=-1, keepdims=True)
    return d * lax.rsqrt(var + LN_EPS) * g + b


def _ln_in_kernel(x_ref, meta_ref, g_ref, b_ref, o_ref):
    t = pl.program_id(1)
    g = g_ref[...]
    b = b_ref[...]

    @pl.when(t == 0)
    def _():
        d = o_ref.shape[-1]
        o_ref[0, :PAD, :] = jnp.zeros((PAD, d), o_ref.dtype)
        o_ref[0, PAD:, :] = _layer_norm_rows(meta_ref[...], g, b).astype(o_ref.dtype)

    @pl.when(t > 0)
    def _():
        for r0 in range(0, o_ref.shape[1], LN_ROWS):
            rows = slice(r0, r0 + LN_ROWS)
            o_ref[0, rows, :] = _layer_norm_rows(x_ref[0, rows, :], g, b).astype(o_ref.dtype)


def _ln_in(x, meta, g, b):
    nb, seq, d = x.shape
    nt = seq // CHUNK + 1
    return pl.pallas_call(
        _ln_in_kernel,
        grid=(nb, nt),
        in_specs=[
            pl.BlockSpec((1, CHUNK, d), lambda bi, t: (bi, jnp.maximum(t - 1, 0), 0)),
            pl.BlockSpec((N_META, d), lambda bi, t: (0, 0)),
            pl.BlockSpec((1, d), lambda bi, t: (0, 0)),
            pl.BlockSpec((1, d), lambda bi, t: (0, 0)),
        ],
        out_specs=pl.BlockSpec((1, CHUNK, d), lambda bi, t: (bi, t, 0)),
        out_shape=jax.ShapeDtypeStruct((nb, nt * CHUNK, d), BF16),
        name="ln_in",
    )(x, meta, g.reshape(1, d), b.reshape(1, d))


def _cast_weight_chunk(wc_refs, wb_refs, slot, chunk, kc):
    rows = pl.ds(pl.multiple_of(chunk * kc, kc), kc)
    for wc, wb in zip(wc_refs, wb_refs):
        wb[slot, rows, :] = wc[...].astype(BF16)


def _mm_kernel(*refs, n_w, n_e, epilogue, kc):
    x_ref = refs[0]
    wc_refs = refs[1:1 + n_w]
    e_refs = refs[1 + n_w:1 + n_w + n_e]
    o_ref = refs[1 + n_w + n_e]
    wb_refs = refs[2 + n_w + n_e:]
    jj = pl.program_id(0)
    i = pl.program_id(1)
    n_col_tiles = pl.num_programs(0) - 1

    @pl.when(jj < n_col_tiles)
    def _():
        _cast_weight_chunk(wc_refs, wb_refs, lax.rem(jj, 2), i, kc)

    @pl.when(jj > 0)
    def _():
        slot = lax.rem(jj - 1, 2)
        x = x_ref[...]
        accs = [jnp.dot(x, wb[slot], preferred_element_type=F32) for wb in wb_refs]
        o_ref[...] = epilogue(accs, [e[...] for e in e_refs]).astype(o_ref.dtype)


def _live_row_start(i, tm, live):
    t_rows, row0, seq = live
    tiles_per_seq = seq // tm
    return ((i // tiles_per_seq) * (t_rows // LANES) + row0 // LANES + (i % tiles_per_seq) * (tm // LANES)) * LANES


def _matmul(x, ws, extras, epilogue, out_dtype, tm, tn, name, live=None):
    k = x.shape[1]
    if live is None:
        r = x.shape[0]
    else:
        t_rows, row0, seq = live
        assert x.shape[0] % t_rows == 0 and seq % tm == 0 and tm % LANES == 0
        assert t_rows % LANES == 0 and row0 % LANES == 0
        r = x.shape[0] // t_rows * seq
    n = ws[0][2]
    nj, ni = n // tn, r // tm
    kc = k // ni
    assert k % ni == 0 and kc % 16 == 0 and n % tn == 0 and r % tm == 0
    assert all(c0 % LANES == 0 for _, c0, _ in ws)
    row = lambda jj, i: jnp.where(jj == 0, 0, i)
    col = lambda jj: jnp.maximum(jj - 1, 0)
    if live is None:
        in_specs = [pl.BlockSpec((tm, k), lambda jj, i: (row(jj, i), 0))]
    else:
        in_specs = [pl.BlockSpec((pl.Element(tm), pl.Element(k)),
                                 lambda jj, i: (_live_row_start(row(jj, i), tm, live), 0))]
    in_specs += [pl.BlockSpec((pl.Element(kc), pl.Element(tn)),
                              lambda jj, i, c0=c0: (
                                  i * kc, (c0 // LANES + jnp.minimum(jj, nj - 1) * (tn // LANES)) * LANES))
                 for _, c0, _ in ws]
    args = [x] + [w for w, _, _ in ws]
    for arr, off in extras:
        if arr.shape[0] == 1:
            in_specs.append(pl.BlockSpec((1, tn), lambda jj, i, off=off: (0, col(jj) + off)))
        else:
            in_specs.append(pl.BlockSpec((tm, tn), lambda jj, i, off=off: (row(jj, i), col(jj) + off)))
        args.append(arr)
    return pl.pallas_call(
        functools.partial(_mm_kernel, n_w=len(ws), n_e=len(extras), epilogue=epilogue, kc=kc),
        grid=(nj + 1, ni),
        in_specs=in_specs,
        out_specs=pl.BlockSpec((tm, tn), lambda jj, i: (row(jj, i), col(jj))),
        out_shape=jax.ShapeDtypeStruct((r, n), out_dtype),
        scratch_shapes=[pltpu.VMEM((2, k, tn), BF16) for _ in ws],
        compiler_params=pltpu.CompilerParams(
            dimension_semantics=("arbitrary", "arbitrary"), vmem_limit_bytes=VMEM_LIMIT),
        name=name,
    )(*args)


def _mm_conv_kernel(*refs, n_w, taps, halo, tm, tn, n_row_tiles, n_tiles, tiles_per_seq, act, row_group, kc):
    x_ref = refs[0]
    wc_refs = refs[1:1 + n_w]
    cw_ref, cb_ref, o_ref = refs[1 + n_w:4 + n_w]
    wb_refs = refs[4 + n_w:4 + 2 * n_w]
    vbuf, stage = refs[4 + 2 * n_w:]
    step = pl.program_id(0)
    m = step - n_row_tiles
    stride = tm // SUBLANES
    slabs_per_group = MXU_COLS // LANES
    n_groups = tn // MXU_COLS

    @pl.when(step == 0)
    def _():
        vbuf[...] = jnp.zeros(vbuf.shape, F32)

    @pl.when(step < n_tiles)
    def _():
        _cast_weight_chunk(wc_refs, wb_refs, lax.rem(step // n_row_tiles, 2), lax.rem(step, n_row_tiles), kc)

    @pl.when(m >= 0)
    def _():
        cur = jnp.minimum(m, n_tiles - 1)
        slot = lax.rem(cur // n_row_tiles, 2)
        starts_seq = lax.rem(lax.rem(cur, n_row_tiles), tiles_per_seq) == 0

        def column_group(g, carry):
            col0 = pl.multiple_of(g * MXU_COLS, MXU_COLS)
            for s2 in range(slabs_per_group):
                s = g * slabs_per_group + s2
                lanes = pl.ds(pl.multiple_of(col0 + s2 * LANES, LANES), LANES)
                for i0 in range(0, stride, row_group):
                    accs = [cb_ref[:, lanes]] * row_group
                    for k in range(taps):
                        wk = cw_ref[k * SUBLANES:(k + 1) * SUBLANES, lanes]
                        for ii in range(row_group):
                            start = halo + i0 + ii - (taps - 1) + k
                            accs[ii] = accs[ii] + vbuf[s, pl.ds(start, SUBLANES, stride=stride), :] * wk
                    for ii in range(row_group):
                        stage[s, pl.ds(i0 + ii, SUBLANES, stride=stride), :] = act(accs[ii])
                o_ref[:, lanes] = stage[s].astype(o_ref.dtype)

            x = x_ref[...]
            cols = pl.ds(col0, MXU_COLS)
            if n_w == 2:
                gate = _sigmoid(jnp.dot(x, wb_refs[1][slot, :, cols], preferred_element_type=F32))
                for s2 in range(slabs_per_group):
                    stage[g * slabs_per_group + s2] = gate[:, s2 * LANES:(s2 + 1) * LANES]
            v = jnp.dot(x, wb_refs[0][slot, :, cols], preferred_element_type=F32)
            for s2 in range(slabs_per_group):
                s = g * slabs_per_group + s2
                vbuf[s, 0:halo, :] = jnp.where(starts_seq, 0.0, vbuf[s, tm:tm + halo, :])
                v_slab = v[:, s2 * LANES:(s2 + 1) * LANES]
                vbuf[s, halo:halo + tm, :] = v_slab * stage[s] if n_w == 2 else v_slab
            return carry
        lax.fori_loop(0, n_groups, column_group, 0)


def _matmul_conv(x, ws, conv_w, conv_b, act, tm, tn, tiles_per_seq, name):
    r, k = x.shape
    n_cols = ws[0][2]
    taps = conv_w.shape[0]
    halo = -(-(taps - 1) // SUBLANES) * SUBLANES
    n_row_tiles = r // tm
    n_col_tiles = n_cols // tn
    n_tiles = n_row_tiles * n_col_tiles
    kc = k // n_row_tiles
    assert tm % (2 * SUBLANES) == 0 and tn % MXU_COLS == 0 and k % n_row_tiles == 0 and kc % 16 == 0
    assert all(c0 % LANES == 0 for _, c0, _ in ws)
    row_group = 12
    assert (tm // SUBLANES) % row_group == 0
    rep = lambda a: jnp.broadcast_to(a[:, None, :], (a.shape[0], SUBLANES, n_cols)).reshape(-1, n_cols)

    def cur_tile(step):
        c = jnp.clip(step - n_row_tiles, 0, n_tiles - 1)
        return c % n_row_tiles, c // n_row_tiles

    def lag_tile(step):
        c = jnp.clip(step - n_row_tiles - 1, 0, n_tiles - 1)
        return c % n_row_tiles, c // n_row_tiles

    def chunk(step, c0):
        tile_col = jnp.minimum(step // n_row_tiles, n_col_tiles - 1)
        return (step % n_row_tiles) * kc, (c0 // LANES + tile_col * (tn // LANES)) * LANES

    in_specs = [pl.BlockSpec((tm, k), lambda n: (cur_tile(n)[0], 0))]
    in_specs += [pl.BlockSpec((pl.Element(kc), pl.Element(tn)), lambda n, c0=c0: chunk(n, c0))
                 for _, c0, _ in ws]
    in_specs += [pl.BlockSpec((taps * SUBLANES, tn), lambda n: (0, lag_tile(n)[1])),
                 pl.BlockSpec((SUBLANES, tn), lambda n: (0, lag_tile(n)[1]))]
    return pl.pallas_call(
        functools.partial(_mm_conv_kernel, n_w=len(ws), taps=taps, halo=halo, tm=tm, tn=tn,
                          n_row_tiles=n_row_tiles, n_tiles=n_tiles, tiles_per_seq=tiles_per_seq, act=act,
                          row_group=row_group, kc=kc),
        grid=(n_row_tiles + n_tiles + 1,),
        in_specs=in_specs,
        out_specs=pl.BlockSpec((tm, tn), lambda n: lag_tile(n)),
        out_shape=jax.ShapeDtypeStruct((r, n_cols), BF16),
        scratch_shapes=[pltpu.VMEM((2, k, tn), BF16) for _ in ws] + [
            pltpu.VMEM((tn // LANES, halo + tm, LANES), F32),
            pltpu.VMEM((tn // LANES, tm, LANES), F32)],
        compiler_params=pltpu.CompilerParams(
            dimension_semantics=("arbitrary",), vmem_limit_bytes=VMEM_LIMIT),
        name=name,
    )(x, *[w for w, _, _ in ws], rep(conv_w), rep(conv_b.reshape(1, n_cols)))


def _dt_kernel(x_ref, w_ref, bias_ref, o_ref, ot_ref, *, tm, t_rows):
    i = pl.program_id(0)
    raw = jnp.dot(x_ref[...], w_ref[...], preferred_element_type=F32)
    dt = _softplus(raw + bias_ref[...])
    tiles_per_seq = t_rows // tm
    row_in_seq = lax.rem(i, tiles_per_seq) * tm + lax.broadcasted_iota(jnp.int32, dt.shape, 0)
    dt = jnp.where(row_in_seq < PAD, 0.0, dt)
    o_ref[...] = dt
    ot_ref[...] = dt.T


def _dt_proj(h2, w_dt, dt_bias, tm, t_rows):
    r, k = h2.shape
    nh = w_dt.shape[1]
    assert t_rows % tm == 0 and tm % LANES == 0
    return pl.pallas_call(
        functools.partial(_dt_kernel, tm=tm, t_rows=t_rows),
        grid=(r // tm,),
        in_specs=[
            pl.BlockSpec((tm, k), lambda i: (i, 0)),
            pl.BlockSpec((k, nh), lambda i: (0, 0)),
            pl.BlockSpec((1, nh), lambda i: (0, 0)),
        ],
        out_specs=[pl.BlockSpec((tm, nh), lambda i: (i, 0)),
                   pl.BlockSpec((nh, tm), lambda i: (0, i))],
        out_shape=[jax.ShapeDtypeStruct((r, nh), F32), jax.ShapeDtypeStruct((nh, r), F32)],
        compiler_params=pltpu.CompilerParams(
            dimension_semantics=("arbitrary",), vmem_limit_bytes=VMEM_LIMIT),
        name="dt_proj",
    )(h2, w_dt, dt_bias.reshape(1, nh))


def _ln_act_kernel(cv_ref, sg_ref, g_ref, b_ref, o_ref):
    half_g = 0.5 * g_ref[...]
    half_b = 0.5 * b_ref[...]
    for r0 in range(0, o_ref.shape[0], LN_ROWS):
        rows = slice(r0, r0 + LN_ROWS)
        half_y = _layer_norm_rows(cv_ref[rows, :].astype(F32), half_g, half_b)
        o_ref[rows, :] = (_silu_of_half(half_y) * sg_ref[rows, :].astype(F32)).astype(o_ref.dtype)


def _ln_act(cv, sg, ln_g, ln_b, tl, live):
    r, c = sg.shape
    assert live[2] % tl == 0 and tl % LANES == 0
    vec = pl.BlockSpec((1, c), lambda i: (0, 0))
    tile = pl.BlockSpec((tl, c), lambda i: (i, 0))
    cv_tile = pl.BlockSpec((pl.Element(tl), pl.Element(c)), lambda i: (_live_row_start(i, tl, live), 0))
    return pl.pallas_call(
        _ln_act_kernel,
        grid=(r // tl,),
        in_specs=[cv_tile, tile, vec, vec],
        out_specs=tile,
        out_shape=jax.ShapeDtypeStruct((r, c), BF16),
        compiler_params=pltpu.CompilerParams(
            dimension_semantics=("arbitrary",), vmem_limit_bytes=VMEM_LIMIT),
        name="conv_ln_act",
    )(cv, sg, ln_g.reshape(1, c), ln_b.reshape(1, c))


def _ssd_kernel(xbc_ref, dt_ref, dtt_ref, sz_ref, alog_ref, alogt_ref, dskip_ref, ng_ref,
                o_ref, state, *, inner):
    c = pl.program_id(1)
    q = CHUNK
    n = SSD_STATE
    n_groups = state.shape[0]
    gw = inner // n_groups
    pair_w = 2 * SSD_HEAD_DIM
    n_pairs = gw // pair_w
    heads_per_group = gw // SSD_HEAD_DIM

    @pl.when(c == 0)
    def _():
        state[...] = jnp.zeros(state.shape, F32)

    dt = dt_ref[0]
    dtt = dtt_ref[...]
    a_row = -jnp.exp(alog_ref[...])
    a_col = -jnp.exp(alogt_ref[...])

    rows = lax.broadcasted_iota(jnp.int32, (q, q), 0)
    cols = lax.broadcasted_iota(jnp.int32, (q, q), 1)
    causal = cols <= rows
    tri = jnp.where(causal, 1.0, 0.0).astype(F32)
    tri_t = jnp.where(rows <= cols, 1.0, 0.0).astype(F32)
    a_cs = jnp.dot(tri, dt * a_row, preferred_element_type=F32,
                   precision=lax.Precision.HIGHEST)
    a_cs_t = jnp.dot(dtt * a_col, tri_t, preferred_element_type=F32,
                     precision=lax.Precision.HIGHEST)
    cs2 = a_cs * LOG2_E
    cs2_t = a_cs_t * LOG2_E
    src2_t = cs2_t - jnp.log2(dtt)
    w_state_t = dtt * jnp.exp2(cs2_t[:, q - 1:q] - cs2_t)

    lane = lax.broadcasted_iota(jnp.int32, (q, pair_w), 1)
    first = lane < SSD_HEAD_DIM

    for g in range(n_groups):
        bm_b = xbc_ref[0, :, inner + g * n:inner + (g + 1) * n]
        cm_b = xbc_ref[0, :, inner + (n_groups + g) * n:inner + (n_groups + g + 1) * n]
        cb = lax.dot_general(cm_b, bm_b, (((1,), (1,)), ((), ())), preferred_element_type=F32)
        bm_t = bm_b.astype(F32).T

        y_parts = []
        for j in range(n_pairs):
            h0 = g * heads_per_group + 2 * j
            h1 = h0 + 1
            lo, hi = g * gw + j * pair_w, g * gw + (j + 1) * pair_w
            ac0 = jnp.broadcast_to(cs2[:, h0:h0 + 1], (q, q))
            ac1 = jnp.broadcast_to(cs2[:, h1:h1 + 1], (q, q))
            ac_pair = jnp.where(first, ac0, ac1)
            tot_pair = ac_pair[q - 1:q, :]

            x_b = xbc_ref[0, :, lo:hi]
            zero = jnp.zeros_like(x_b)
            x_bd = jnp.concatenate([jnp.where(first, x_b, zero), jnp.where(first, zero, x_b)], axis=0)

            seg0 = jnp.where(causal, ac0 - src2_t[h0:h0 + 1, :], -jnp.inf)
            seg1 = jnp.where(causal, ac1 - src2_t[h1:h1 + 1, :], -jnp.inf)
            m0 = (cb * jnp.exp2(seg0)).astype(BF16)
            m1 = (cb * jnp.exp2(seg1)).astype(BF16)
            y_diag = jnp.dot(jnp.concatenate([m0, m1], axis=1), x_bd, preferred_element_type=F32)

            st = state[g, :, j * pair_w:(j + 1) * pair_w]
            y_off = jnp.dot(cm_b, st.astype(BF16), preferred_element_type=F32) * jnp.exp2(ac_pair)
            y_parts.append(y_diag + y_off + dskip_ref[:, lo:hi] * x_b.astype(F32))

            w0 = (bm_t * w_state_t[h0:h0 + 1, :]).astype(BF16)
            w1 = (bm_t * w_state_t[h1:h1 + 1, :]).astype(BF16)
            upd = jnp.dot(jnp.concatenate([w0, w1], axis=1), x_bd, preferred_element_type=F32)
            state[g, :, j * pair_w:(j + 1) * pair_w] = st * jnp.exp2(tot_pair) + upd

        y = jnp.concatenate(y_parts, axis=1)
        yz = y * sz_ref[0, :, g * gw:(g + 1) * gw].astype(F32)
        ms = jnp.mean(yz * yz, axis=-1, keepdims=True)
        o_ref[0, :, g * gw:(g + 1) * gw] = (
            yz * lax.rsqrt(ms + RMS_EPS) * ng_ref[:, g * gw:(g + 1) * gw]).astype(o_ref.dtype)


def _ssd_branch(xbc3, dt3, dt_t, sz3, a_log, d_skip, norm_g):
    nb, t_rows, xbc_w = xbc3.shape
    inner = sz3.shape[-1]
    heads = a_log.shape[0]
    nc = t_rows // CHUNK
    dskip_x = jnp.repeat(d_skip, SSD_HEAD_DIM).reshape(1, inner)
    full = lambda shape: pl.BlockSpec(shape, lambda bi, ci: (0,) * len(shape))

    return pl.pallas_call(
        functools.partial(_ssd_kernel, inner=inner),
        grid=(nb, nc),
        in_specs=[
            pl.BlockSpec((1, CHUNK, xbc_w), lambda bi, ci: (bi, ci, 0)),
            pl.BlockSpec((1, CHUNK, heads), lambda bi, ci: (bi, ci, 0)),
            pl.BlockSpec((heads, CHUNK), lambda bi, ci: (0, bi * nc + ci)),
            pl.BlockSpec((1, CHUNK, inner), lambda bi, ci: (bi, ci, 0)),
            full((1, heads)), full((heads, 1)), full((1, inner)), full((1, inner)),
        ],
        out_specs=pl.BlockSpec((1, CHUNK, inner), lambda bi, ci: (bi, ci, 0)),
        out_shape=jax.ShapeDtypeStruct((nb, t_rows, inner), BF16),
        scratch_shapes=[pltpu.VMEM((SSD_GROUPS, SSD_STATE, inner // SSD_GROUPS), F32)],
        compiler_params=pltpu.CompilerParams(
            dimension_semantics=("arbitrary", "arbitrary"), vmem_limit_bytes=VMEM_LIMIT),
        name="ssd_branch",
    )(xbc3, dt3, dt_t, sz3, a_log.reshape(1, heads), a_log.reshape(heads, 1), dskip_x,
      norm_g.reshape(1, inner))


def _ln_out_kernel(x_ref, sub_ref, gi_ref, bi_ref, go_ref, bo_ref, o_ref):
    h = _layer_norm_rows(x_ref[0], gi_ref[...], bi_ref[...])
    o_ref[0] = _layer_norm_rows(DEEPNORM_ALPHA * h + sub_ref[0].astype(F32), go_ref[...], bo_ref[...])


def _ln_out(x, sub3, gi, bi, go, bo):
    nb, seq, d = x.shape
    row = lambda a: a.reshape(1, d)
    vec = pl.BlockSpec((1, d), lambda b_, t: (0, 0))
    return pl.pallas_call(
        _ln_out_kernel,
        grid=(nb, seq // CHUNK),
        in_specs=[
            pl.BlockSpec((1, CHUNK, d), lambda b_, t: (b_, t, 0)),
            pl.BlockSpec((1, CHUNK, d), lambda b_, t: (b_, t, 0)),
            vec, vec, vec, vec,
        ],
        out_specs=pl.BlockSpec((1, CHUNK, d), lambda b_, t: (b_, t, 0)),
        out_shape=jax.ShapeDtypeStruct((nb, seq, d), F32),
        name="ln_out",
    )(x, sub3, row(gi), row(bi), row(go), row(bo))


def kernel(x, meta_tokens, ln_in_g, ln_in_b, w_in, b_gate, conv_w, conv_b, conv_ln_g, conv_ln_b,
           w_conv_out, ssd_conv_w, ssd_conv_b, dt_bias, a_log, d_skip, ssd_norm_g, w_ssd_out,
           w_out, ln_out_g, ln_out_b):
    nb, seq, d = x.shape
    assert w_in.shape[0] == 1, "single-layer trunk"
    assert seq % CHUNK == 0
    t_rows = seq + CHUNK
    r = nb * t_rows
    inner = w_ssd_out.shape[1]
    heads = a_log.shape[1]
    xbc_w = ssd_conv_w.shape[2]
    tiles_per_seq = 4
    tm = t_rows // tiles_per_seq
    tn = 1024
    assert t_rows % tiles_per_seq == 0 and tm % 16 == 0

    o_val, o_glu, o_gate, o_z = 0, d, 2 * d, 3 * d
    o_xbc = o_z + inner
    o_dt = o_xbc + xbc_w
    o_g = o_dt + heads
    w = w_in[0]

    h3 = _ln_in(x, meta_tokens, ln_in_g, ln_in_b)
    h2 = h3.reshape(r, d)

    cv = _matmul_conv(h2, [(w, o_val, d), (w, o_glu, d)], conv_w[0], conv_b[0],
                      lambda a: a, tm, tn // 2, tiles_per_seq, "in_proj_glu_conv")
    live = (t_rows, CHUNK, seq)
    tml = seq // tiles_per_seq
    sg = _matmul(h2, [(w, o_gate, d)], [], lambda a, e: _silu(a[0]), BF16, tml, tn, "in_proj_gate", live)
    sz = _matmul(h2, [(w, o_z, inner)], [], lambda a, e: _silu(a[0]), BF16, tm, tn, "in_proj_z")
    xbc = _matmul_conv(h2, [(w, o_xbc, xbc_w)], 0.5 * ssd_conv_w[0], 0.5 * ssd_conv_b[0],
                       _silu_of_half, tm, tn, tiles_per_seq, "in_proj_xbc_conv")
    dt, dt_t = _dt_proj(h2, w[:, o_dt:o_g].astype(BF16), dt_bias[0], t_rows // 3, t_rows)
    gates = _matmul(h2, [(w, o_g, 2 * d)], [(b_gate[0].reshape(1, 2 * d), 0)],
                    lambda a, e: _sigmoid(a[0] + e[0]), BF16, tml, tn, "in_proj_merge_gates", live)

    u = _ln_act(cv, sg, conv_ln_g[0], conv_ln_b[0], 2 * CHUNK, live)
    yn = _ssd_branch(xbc.reshape(nb, t_rows, xbc_w), dt.reshape(nb, t_rows, heads), dt_t,
                     sz.reshape(nb, t_rows, inner), a_log[0], d_skip[0], ssd_norm_g[0])

    gated_conv = _matmul(u, [(w_conv_out[0], 0, d)], [(gates, 0)],
                         lambda a, e: a[0] * e[0].astype(F32), BF16, tml, tn, "conv_out_proj")
    merged = _matmul(yn.reshape(r, inner), [(w_ssd_out[0], 0, d)],
                     [(gates, 2 * d // tn), (gated_conv, 0)],
                     lambda a, e: a[0] * e[0].astype(F32) + e[1].astype(F32),
                     BF16, tml // 2, tn // 2, "ssd_out_proj_merge", live)
    sub = _matmul(merged, [(w_out[0], 0, d)], [], lambda a, e: a[0], F32, tml, tn, "out_proj")

    return _ln_out(x, sub.reshape(nb, seq, d), ln_in_g, ln_in_b, ln_out_g[0], ln_out_b[0])
```

```python
import functools

import jax
import jax.numpy as jnp
from jax import lax
from jax.experimental import pallas as pl
from jax.experimental.pallas import tpu as pltpu

F32 = jnp.float32
BF16 = jnp.bfloat16

N_META = 16
CHUNK = 128
PAD = CHUNK - N_META
CONV_WIDTH = 31
SSD_HEAD_DIM = 64
SSD_GROUPS = 8
SSD_STATE = 128
SSD_CONV_WIDTH = 4
LN_EPS = 1e-5
RMS_EPS = 1e-5
DEEPNORM_ALPHA = 2.0 ** 0.25
LOG2_E = 1.4426950408889634

SUBLANES = 8
LANES = 128
MXU_COLS = 256
LN_ROWS = 16
VMEM_LIMIT = 56 * 1024 * 1024


def _sigmoid(x):
    return 0.5 * jnp.tanh(0.5 * x) + 0.5


def _silu_of_half(h):
    return h + h * jnp.tanh(h)


def _silu(x):
    return _silu_of_half(0.5 * x)


def _softplus(x):
    return jnp.maximum(x, 0.0) + jnp.log1p(jnp.exp(-jnp.abs(x)))


def _layer_norm_rows(v, g, b):
    mu = jnp.mean(v, axis=-1, keepdims=True)
    d = v - mu
    var = jnp.mean(d * d, axis=-1, keepdims=True)
    return d * lax.rsqrt(var + LN_EPS) * g + b


def _ln_in_kernel(x_ref, meta_ref, g_ref, b_ref, o_ref):
    t = pl.program_id(1)
    g = g_ref[...]
    b = b_ref[...]

    @pl.when(t == 0)
    def _():
        d = o_ref.shape[-1]
        o_ref[0, :PAD, :] = jnp.zeros((PAD, d), o_ref.dtype)
        o_ref[0, PAD:, :] = _layer_norm_rows(meta_ref[...], g, b).astype(o_ref.dtype)

    @pl.when(t > 0)
    def _():
        for r0 in range(0, o_ref.shape[1], LN_ROWS):
            rows = slice(r0, r0 + LN_ROWS)
            o_ref[0, rows, :] = _layer_norm_rows(x_ref[0, rows, :], g, b).astype(o_ref.dtype)


def _ln_in(x, meta, g, b):
    nb, seq, d = x.shape
    nt = seq // CHUNK + 1
    return pl.pallas_call(
        _ln_in_kernel,
        grid=(nb, nt),
        in_specs=[
            pl.BlockSpec((1, CHUNK, d), lambda bi, t: (bi, jnp.maximum(t - 1, 0), 0)),
            pl.BlockSpec((N_META, d), lambda bi, t: (0, 0)),
            pl.BlockSpec((1, d), lambda bi, t: (0, 0)),
            pl.BlockSpec((1, d), lambda bi, t: (0, 0)),
        ],
        out_specs=pl.BlockSpec((1, CHUNK, d), lambda bi, t: (bi, t, 0)),
        out_shape=jax.ShapeDtypeStruct((nb, nt * CHUNK, d), BF16),
        name="ln_in",
    )(x, meta, g.reshape(1, d), b.reshape(1, d))


def _cast_weight_chunk(wc_refs, wb_refs, slot, chunk, kc):
    rows = pl.ds(pl.multiple_of(chunk * kc, kc), kc)
    for wc, wb in zip(wc_refs, wb_refs):
        wb[slot, rows, :] = wc[...].astype(BF16)


def _mm_kernel(*refs, n_w, n_e, epilogue, kc):
    x_ref = refs[0]
    wc_refs = refs[1:1 + n_w]
    e_refs = refs[1 + n_w:1 + n_w + n_e]
    o_ref = refs[1 + n_w + n_e]
    wb_refs = refs[2 + n_w + n_e:]
    jj = pl.program_id(0)
    i = pl.program_id(1)
    n_col_tiles = pl.num_programs(0) - 1

    @pl.when(jj < n_col_tiles)
    def _():
        _cast_weight_chunk(wc_refs, wb_refs, lax.rem(jj, 2), i, kc)

    @pl.when(jj > 0)
    def _():
        slot = lax.rem(jj - 1, 2)
        x = x_ref[...]
        accs = [jnp.dot(x, wb[slot], preferred_element_type=F32) for wb in wb_refs]
        o_ref[...] = epilogue(accs, [e[...] for e in e_refs]).astype(o_ref.dtype)


def _live_row_start(i, tm, live):
    t_rows, row0, seq = live
    tiles_per_seq = seq // tm
    return ((i // tiles_per_seq) * (t_rows // LANES) + row0 // LANES + (i % tiles_per_seq) * (tm // LANES)) * LANES


def _matmul(x, ws, extras, epilogue, out_dtype, tm, tn, name, live=None):
    k = x.shape[1]
    if live is None:
        r = x.shape[0]
    else:
        t_rows, row0, seq = live
        assert x.shape[0] % t_rows == 0 and seq % tm == 0 and tm % LANES == 0
        assert t_rows % LANES == 0 and row0 % LANES == 0
        r = x.shape[0] // t_rows * seq
    n = ws[0][2]
    nj, ni = n // tn, r // tm
    kc = k // ni
    assert k % ni == 0 and kc % 16 == 0 and n % tn == 0 and r % tm == 0
    assert all(c0 % LANES == 0 for _, c0, _ in ws)
    row = lambda jj, i: jnp.where(jj == 0, 0, i)
    col = lambda jj: jnp.maximum(jj - 1, 0)
    if live is None:
        in_specs = [pl.BlockSpec((tm, k), lambda jj, i: (row(jj, i), 0))]
    else:
        in_specs = [pl.BlockSpec((pl.Element(tm), pl.Element(k)),
                                 lambda jj, i: (_live_row_start(row(jj, i), tm, live), 0))]
    in_specs += [pl.BlockSpec((pl.Element(kc), pl.Element(tn)),
                              lambda jj, i, c0=c0: (
                                  i * kc, (c0 // LANES + jnp.minimum(jj, nj - 1) * (tn // LANES)) * LANES))
                 for _, c0, _ in ws]
    args = [x] + [w for w, _, _ in ws]
    for arr, off in extras:
        if arr.shape[0] == 1:
            in_specs.append(pl.BlockSpec((1, tn), lambda jj, i, off=off: (0, col(jj) + off)))
        else:
            in_specs.append(pl.BlockSpec((tm, tn), lambda jj, i, off=off: (row(jj, i), col(jj) + off)))
        args.append(arr)
    return pl.pallas_call(
        functools.partial(_mm_kernel, n_w=len(ws), n_e=len(extras), epilogue=epilogue, kc=kc),
        grid=(nj + 1, ni),
        in_specs=in_specs,
        out_specs=pl.BlockSpec((tm, tn), lambda jj, i: (row(jj, i), col(jj))),
        out_shape=jax.ShapeDtypeStruct((r, n), out_dtype),
        scratch_shapes=[pltpu.VMEM((2, k, tn), BF16) for _ in ws],
        compiler_params=pltpu.CompilerParams(
            dimension_semantics=("arbitrary", "arbitrary"), vmem_limit_bytes=VMEM_LIMIT),
        name=name,
    )(*args)


def _mm_conv_kernel(*refs, n_w, taps, halo, tm, tn, n_row_tiles, n_tiles, tiles_per_seq, act, row_group, kc):
    x_ref = refs[0]
    wc_refs = refs[1:1 + n_w]
    cw_ref, cb_ref, o_ref = refs[1 + n_w:4 + n_w]
    wb_refs = refs[4 + n_w:4 + 2 * n_w]
    vbuf, stage = refs[4 + 2 * n_w:]
    step = pl.program_id(0)
    m = step - n_row_tiles
    stride = tm // SUBLANES
    slabs_per_group = MXU_COLS // LANES
    n_groups = tn // MXU_COLS

    @pl.when(step == 0)
    def _():
        vbuf[...] = jnp.zeros(vbuf.shape, F32)

    @pl.when(step < n_tiles)
    def _():
        _cast_weight_chunk(wc_refs, wb_refs, lax.rem(step // n_row_tiles, 2), lax.rem(step, n_row_tiles), kc)

    @pl.when(m >= 0)
    def _():
        cur = jnp.minimum(m, n_tiles - 1)
        slot = lax.rem(cur // n_row_tiles, 2)
        starts_seq = lax.rem(lax.rem(cur, n_row_tiles), tiles_per_seq) == 0

        def column_group(g, carry):
            col0 = pl.multiple_of(g * MXU_COLS, MXU_COLS)
            for s2 in range(slabs_per_group):
                s = g * slabs_per_group + s2
                lanes = pl.ds(pl.multiple_of(col0 + s2 * LANES, LANES), LANES)
                for i0 in range(0, stride, row_group):
                    accs = [cb_ref[:, lanes]] * row_group
                    for k in range(taps):
                        wk = cw_ref[k * SUBLANES:(k + 1) * SUBLANES, lanes]
                        for ii in range(row_group):
                            start = halo + i0 + ii - (taps - 1) + k
                            accs[ii] = accs[ii] + vbuf[s, pl.ds(start, SUBLANES, stride=stride), :] * wk
                    for ii in range(row_group):
                        stage[s, pl.ds(i0 + ii, SUBLANES, stride=stride), :] = act(accs[ii])
                o_ref[:, lanes] = stage[s].astype(o_ref.dtype)

            x = x_ref[...]
            cols = pl.ds(col0, MXU_COLS)
            if n_w == 2:
                gate = _sigmoid(jnp.dot(x, wb_refs[1][slot, :, cols], preferred_element_type=F32))
                for s2 in range(slabs_per_group):
                    stage[g * slabs_per_group + s2] = gate[:, s2 * LANES:(s2 + 1) * LANES]
            v = jnp.dot(x, wb_refs[0][slot, :, cols], preferred_element_type=F32)
            for s2 in range(slabs_per_group):
                s = g * slabs_per_group + s2
                vbuf[s, 0:halo, :] = jnp.where(starts_seq, 0.0, vbuf[s, tm:tm + halo, :])
                v_slab = v[:, s2 * LANES:(s2 + 1) * LANES]
                vbuf[s, halo:halo + tm, :] = v_slab * stage[s] if n_w == 2 else v_slab
            return carry
        lax.fori_loop(0, n_groups, column_group, 0)


def _matmul_conv(x, ws, conv_w, conv_b, act, tm, tn, tiles_per_seq, name):
    r, k = x.shape
    n_cols = ws[0][2]
    taps = conv_w.shape[0]
    halo = -(-(taps - 1) // SUBLANES) * SUBLANES
    n_row_tiles = r // tm
    n_col_tiles = n_cols // tn
    n_tiles = n_row_tiles * n_col_tiles
    kc = k // n_row_tiles
    assert tm % (2 * SUBLANES) == 0 and tn % MXU_COLS == 0 and k % n_row_tiles == 0 and kc % 16 == 0
    assert all(c0 % LANES == 0 for _, c0, _ in ws)
    row_group = 12
    assert (tm // SUBLANES) % row_group == 0
    rep = lambda a: jnp.broadcast_to(a[:, None, :], (a.shape[0], SUBLANES, n_cols)).reshape(-1, n_cols)

    def cur_tile(step):
        c = jnp.clip(step - n_row_tiles, 0, n_tiles - 1)
        return c % n_row_tiles, c // n_row_tiles

    def lag_tile(step):
        c = jnp.clip(step - n_row_tiles - 1, 0, n_tiles - 1)
        return c % n_row_tiles, c // n_row_tiles

    def chunk(step, c0):
        tile_col = jnp.minimum(step // n_row_tiles, n_col_tiles - 1)
        return (step % n_row_tiles) * kc, (c0 // LANES + tile_col * (tn // LANES)) * LANES

    in_specs = [pl.BlockSpec((tm, k), lambda n: (cur_tile(n)[0], 0))]
    in_specs += [pl.BlockSpec((pl.Element(kc), pl.Element(tn)), lambda n, c0=c0: chunk(n, c0))
                 for _, c0, _ in ws]
    in_specs += [pl.BlockSpec((taps * SUBLANES, tn), lambda n: (0, lag_tile(n)[1])),
                 pl.BlockSpec((SUBLANES, tn), lambda n: (0, lag_tile(n)[1]))]
    return pl.pallas_call(
        functools.partial(_mm_conv_kernel, n_w=len(ws), taps=taps, halo=halo, tm=tm, tn=tn,
                          n_row_tiles=n_row_tiles, n_tiles=n_tiles, tiles_per_seq=tiles_per_seq, act=act,
                          row_group=row_group, kc=kc),
        grid=(n_row_tiles + n_tiles + 1,),
        in_specs=in_specs,
        out_specs=pl.BlockSpec((tm, tn), lambda n: lag_tile(n)),
        out_shape=jax.ShapeDtypeStruct((r, n_cols), BF16),
        scratch_shapes=[pltpu.VMEM((2, k, tn), BF16) for _ in ws] + [
            pltpu.VMEM((tn // LANES, halo + tm, LANES), F32),
            pltpu.VMEM((tn // LANES, tm, LANES), F32)],
        compiler_params=pltpu.CompilerParams(
            dimension_semantics=("arbitrary",), vmem_limit_bytes=VMEM_LIMIT),
        name=name,
    )(x, *[w for w, _, _ in ws], rep(conv_w), rep(conv_b.reshape(1, n_cols)))


def _dt_kernel(x_ref, w_ref, bias_ref, o_ref, ot_ref, *, tm, t_rows):
    i = pl.program_id(0)
    raw = jnp.dot(x_ref[...], w_ref[...], preferred_element_type=F32)
    dt = _softplus(raw + bias_ref[...])
    tiles_per_seq = t_rows // tm
    row_in_seq = lax.rem(i, tiles_per_seq) * tm + lax.broadcasted_iota(jnp.int32, dt.shape, 0)
    dt = jnp.where(row_in_seq < PAD, 0.0, dt)
    o_ref[...] = dt
    ot_ref[...] = dt.T


def _dt_proj(h2, w_dt, dt_bias, tm, t_rows):
    r, k = h2.shape
    nh = w_dt.shape[1]
    assert t_rows % tm == 0 and tm % LANES == 0
    return pl.pallas_call(
        functools.partial(_dt_kernel, tm=tm, t_rows=t_rows),
        grid=(r // tm,),
        in_specs=[
            pl.BlockSpec((tm, k), lambda i: (i, 0)),
            pl.BlockSpec((k, nh), lambda i: (0, 0)),
            pl.BlockSpec((1, nh), lambda i: (0, 0)),
        ],
        out_specs=[pl.BlockSpec((tm, nh), lambda i: (i, 0)),
                   pl.BlockSpec((nh, tm), lambda i: (0, i))],
        out_shape=[jax.ShapeDtypeStruct((r, nh), F32), jax.ShapeDtypeStruct((nh, r), F32)],
        compiler_params=pltpu.CompilerParams(
            dimension_semantics=("arbitrary",), vmem_limit_bytes=VMEM_LIMIT),
        name="dt_proj",
    )(h2, w_dt, dt_bias.reshape(1, nh))


def _ln_act_kernel(cv_ref, sg_ref, g_ref, b_ref, o_ref):
    half_g = 0.5 * g_ref[...]
    half_b = 0.5 * b_ref[...]
    for r0 in range(0, o_ref.shape[0], LN_ROWS):
        rows = slice(r0, r0 + LN_ROWS)
        half_y = _layer_norm_rows(cv_ref[rows, :].astype(F32), half_g, half_b)
        o_ref[rows, :] = (_silu_of_half(half_y) * sg_ref[rows, :].astype(F32)).astype(o_ref.dtype)


def _ln_act(cv, sg, ln_g, ln_b, tl, live):
    r, c = sg.shape
    assert live[2] % tl == 0 and tl % LANES == 0
    vec = pl.BlockSpec((1, c), lambda i: (0, 0))
    tile = pl.BlockSpec((tl, c), lambda i: (i, 0))
    cv_tile = pl.BlockSpec((pl.Element(tl), pl.Element(c)), lambda i: (_live_row_start(i, tl, live), 0))
    return pl.pallas_call(
        _ln_act_kernel,
        grid=(r // tl,),
        in_specs=[cv_tile, tile, vec, vec],
        out_specs=tile,
        out_shape=jax.ShapeDtypeStruct((r, c), BF16),
        compiler_params=pltpu.CompilerParams(
            dimension_semantics=("arbitrary",), vmem_limit_bytes=VMEM_LIMIT),
        name="conv_ln_act",
    )(cv, sg, ln_g.reshape(1, c), ln_b.reshape(1, c))


def _ssd_kernel(xbc_ref, dt_ref, dtt_ref, sz_ref, alog_ref, alogt_ref, dskip_ref, ng_ref,
                o_ref, state, *, inner):
    c = pl.program_id(1)
    q = CHUNK
    n = SSD_STATE
    n_groups = state.shape[0]
    gw = inner // n_groups
    pair_w = 2 * SSD_HEAD_DIM
    n_pairs = gw // pair_w
    heads_per_group = gw // SSD_HEAD_DIM

    @pl.when(c == 0)
    def _():
        state[...] = jnp.zeros(state.shape, F32)

    dt = dt_ref[0]
    dtt = dtt_ref[...]
    a_row = -jnp.exp(alog_ref[...])
    a_col = -jnp.exp(alogt_ref[...])

    rows = lax.broadcasted_iota(jnp.int32, (q, q), 0)
    cols = lax.broadcasted_iota(jnp.int32, (q, q), 1)
    causal = cols <= rows
    tri = jnp.where(causal, 1.0, 0.0).astype(F32)
    tri_t = jnp.where(rows <= cols, 1.0, 0.0).astype(F32)
    a_cs = jnp.dot(tri, dt * a_row, preferred_element_type=F32,
                   precision=lax.Precision.HIGHEST)
    a_cs_t = jnp.dot(dtt * a_col, tri_t, preferred_element_type=F32,
                     precision=lax.Precision.HIGHEST)
    cs2 = a_cs * LOG2_E
    cs2_t = a_cs_t * LOG2_E
    src2_t = cs2_t - jnp.log2(dtt)
    w_state_t = dtt * jnp.exp2(cs2_t[:, q - 1:q] - cs2_t)

    lane = lax.broadcasted_iota(jnp.int32, (q, pair_w), 1)
    first = lane < SSD_HEAD_DIM

    for g in range(n_groups):
        bm_b = xbc_ref[0, :, inner + g * n:inner + (g + 1) * n]
        cm_b = xbc_ref[0, :, inner + (n_groups + g) * n:inner + (n_groups + g + 1) * n]
        cb = lax.dot_general(cm_b, bm_b, (((1,), (1,)), ((), ())), preferred_element_type=F32)
        bm_t = bm_b.astype(F32).T

        y_parts = []
        for j in range(n_pairs):
            h0 = g * heads_per_group + 2 * j
            h1 = h0 + 1
            lo, hi = g * gw + j * pair_w, g * gw + (j + 1) * pair_w
            ac0 = jnp.broadcast_to(cs2[:, h0:h0 + 1], (q, q))
            ac1 = jnp.broadcast_to(cs2[:, h1:h1 + 1], (q, q))
            ac_pair = jnp.where(first, ac0, ac1)
            tot_pair = ac_pair[q - 1:q, :]

            x_b = xbc_ref[0, :, lo:hi]
            zero = jnp.zeros_like(x_b)
            x_bd = jnp.concatenate([jnp.where(first, x_b, zero), jnp.where(first, zero, x_b)], axis=0)

            seg0 = jnp.where(causal, ac0 - src2_t[h0:h0 + 1, :], -jnp.inf)
            seg1 = jnp.where(causal, ac1 - src2_t[h1:h1 + 1, :], -jnp.inf)
            m0 = (cb * jnp.exp2(seg0)).astype(BF16)
            m1 = (cb * jnp.exp2(seg1)).astype(BF16)
            y_diag = jnp.dot(jnp.concatenate([m0, m1], axis=1), x_bd, preferred_element_type=F32)

            st = state[g, :, j * pair_w:(j + 1) * pair_w]
            y_off = jnp.dot(cm_b, st.astype(BF16), preferred_element_type=F32) * jnp.exp2(ac_pair)
            y_parts.append(y_diag + y_off + dskip_ref[:, lo:hi] * x_b.astype(F32))

            w0 = (bm_t * w_state_t[h0:h0 + 1, :]).astype(BF16)
            w1 = (bm_t * w_state_t[h1:h1 + 1, :]).astype(BF16)
            upd = jnp.dot(jnp.concatenate([w0, w1], axis=1), x_bd, preferred_element_type=F32)
            state[g, :, j * pair_w:(j + 1) * pair_w] = st * jnp.exp2(tot_pair) + upd

        y = jnp.concatenate(y_parts, axis=1)
        yz = y * sz_ref[0, :, g * gw:(g + 1) * gw].astype(F32)
        ms = jnp.mean(yz * yz, axis=-1, keepdims=True)
        o_ref[0, :, g * gw:(g + 1) * gw] = (
            yz * lax.rsqrt(ms + RMS_EPS) * ng_ref[:, g * gw:(g + 1) * gw]).astype(o_ref.dtype)


def _ssd_branch(xbc3, dt3, dt_t, sz3, a_log, d_skip, norm_g):
    nb, t_rows, xbc_w = xbc3.shape
    assert sz3.shape[1] == t_rows - CHUNK
    inner = sz3.shape[-1]
    heads = a_log.shape[0]
    nc = t_rows // CHUNK
    dskip_x = jnp.repeat(d_skip, SSD_HEAD_DIM).reshape(1, inner)
    full = lambda shape: pl.BlockSpec(shape, lambda bi, ci: (0,) * len(shape))

    return pl.pallas_call(
        functools.partial(_ssd_kernel, inner=inner),
        grid=(nb, nc),
        in_specs=[
            pl.BlockSpec((1, CHUNK, xbc_w), lambda bi, ci: (bi, ci, 0)),
            pl.BlockSpec((1, CHUNK, heads), lambda bi, ci: (bi, ci, 0)),
            pl.BlockSpec((heads, CHUNK), lambda bi, ci: (0, bi * nc + ci)),
            pl.BlockSpec((1, CHUNK, inner), lambda bi, ci: (bi, jnp.maximum(ci - 1, 0), 0)),
            full((1, heads)), full((heads, 1)), full((1, inner)), full((1, inner)),
        ],
        out_specs=pl.BlockSpec((1, CHUNK, inner), lambda bi, ci: (bi, ci, 0)),
        out_shape=jax.ShapeDtypeStruct((nb, t_rows, inner), BF16),
        scratch_shapes=[pltpu.VMEM((SSD_GROUPS, SSD_STATE, inner // SSD_GROUPS), F32)],
        compiler_params=pltpu.CompilerParams(
            dimension_semantics=("arbitrary", "arbitrary"), vmem_limit_bytes=VMEM_LIMIT),
        name="ssd_branch",
    )(xbc3, dt3, dt_t, sz3, a_log.reshape(1, heads), a_log.reshape(heads, 1), dskip_x,
      norm_g.reshape(1, inner))


def _ln_out_kernel(x_ref, sub_ref, gi_ref, bi_ref, go_ref, bo_ref, o_ref):
    h = _layer_norm_rows(x_ref[0], gi_ref[...], bi_ref[...])
    o_ref[0] = _layer_norm_rows(DEEPNORM_ALPHA * h + sub_ref[0].astype(F32), go_ref[...], bo_ref[...])


def _ln_out(x, sub3, gi, bi, go, bo):
    nb, seq, d = x.shape
    row = lambda a: a.reshape(1, d)
    vec = pl.BlockSpec((1, d), lambda b_, t: (0, 0))
    return pl.pallas_call(
        _ln_out_kernel,
        grid=(nb, seq // CHUNK),
        in_specs=[
            pl.BlockSpec((1, CHUNK, d), lambda b_, t: (b_, t, 0)),
            pl.BlockSpec((1, CHUNK, d), lambda b_, t: (b_, t, 0)),
            vec, vec, vec, vec,
        ],
        out_specs=pl.BlockSpec((1, CHUNK, d), lambda b_, t: (b_, t, 0)),
        out_shape=jax.ShapeDtypeStruct((nb, seq, d), F32),
        name="ln_out",
    )(x, sub3, row(gi), row(bi), row(go), row(bo))


def kernel(x, meta_tokens, ln_in_g, ln_in_b, w_in, b_gate, conv_w, conv_b, conv_ln_g, conv_ln_b,
           w_conv_out, ssd_conv_w, ssd_conv_b, dt_bias, a_log, d_skip, ssd_norm_g, w_ssd_out,
           w_out, ln_out_g, ln_out_b):
    nb, seq, d = x.shape
    assert w_in.shape[0] == 1, "single-layer trunk"
    assert seq % CHUNK == 0
    t_rows = seq + CHUNK
    r = nb * t_rows
    inner = w_ssd_out.shape[1]
    heads = a_log.shape[1]
    xbc_w = ssd_conv_w.shape[2]
    tiles_per_seq = 4
    tm = t_rows // tiles_per_seq
    tn = 1024
    assert t_rows % tiles_per_seq == 0 and tm % 16 == 0

    o_val, o_glu, o_gate, o_z = 0, d, 2 * d, 3 * d
    o_xbc = o_z + inner
    o_dt = o_xbc + xbc_w
    o_g = o_dt + heads
    w = w_in[0]

    h3 = _ln_in(x, meta_tokens, ln_in_g, ln_in_b)
    h2 = h3.reshape(r, d)

    cv = _matmul_conv(h2, [(w, o_val, d), (w, o_glu, d)], conv_w[0], conv_b[0],
                      lambda a: a, tm, tn // 2, tiles_per_seq, "in_proj_glu_conv")
    live = (t_rows, CHUNK, seq)
    tml = seq // tiles_per_seq
    sg = _matmul(h2, [(w, o_gate, d)], [], lambda a, e: _silu(a[0]), BF16, tml, tn, "in_proj_gate", live)
    sz = _matmul(h2, [(w, o_z, inner)], [], lambda a, e: _silu(a[0]), BF16, tml, tn, "in_proj_z", live)
    xbc = _matmul_conv(h2, [(w, o_xbc, xbc_w)], 0.5 * ssd_conv_w[0], 0.5 * ssd_conv_b[0],
                       _silu_of_half, tm, tn, tiles_per_seq, "in_proj_xbc_conv")
    dt, dt_t = _dt_proj(h2, w[:, o_dt:o_g].astype(BF16), dt_bias[0], t_rows // 3, t_rows)
    gates = _matmul(h2, [(w, o_g, 2 * d)], [(b_gate[0].reshape(1, 2 * d), 0)],
                    lambda a, e: _sigmoid(a[0] + e[0]), BF16, tml, tn, "in_proj_merge_gates", live)

    u = _ln_act(cv, sg, conv_ln_g[0], conv_ln_b[0], 2 * CHUNK, live)
    yn = _ssd_branch(xbc.reshape(nb, t_rows, xbc_w), dt.reshape(nb, t_rows, heads), dt_t,
                     sz.reshape(nb, seq, inner), a_log[0], d_skip[0], ssd_norm_g[0])

    gated_conv = _matmul(u, [(w_conv_out[0], 0, d)], [(gates, 0)],
                         lambda a, e: a[0] * e[0].astype(F32), BF16, tml, tn, "conv_out_proj")
    merged = _matmul(yn.reshape(r, inner), [(w_ssd_out[0], 0, d)],
                     [(gates, 2 * d // tn), (gated_conv, 0)],
                     lambda a, e: a[0] * e[0].astype(F32) + e[1].astype(F32),
                     BF16, tml // 2, tn // 2, "ssd_out_proj_merge", live)
    sub = _matmul(merged, [(w_out[0], 0, d)], [], lambda a, e: a[0], F32, tml, tn, "out_proj")

    return _ln_out(x, sub.reshape(nb, seq, d), ln_in_g, ln_in_b, ln_out_g[0], ln_out_b[0])
```

```python
import functools

import jax
import jax.numpy as jnp
from jax import lax
from jax.experimental import pallas as pl
from jax.experimental.pallas import tpu as pltpu

F32 = jnp.float32
BF16 = jnp.bfloat16

N_META = 16
CHUNK = 128
PAD = CHUNK - N_META
CONV_WIDTH = 31
SSD_HEAD_DIM = 64
SSD_GROUPS = 8
SSD_STATE = 128
SSD_CONV_WIDTH = 4
LN_EPS = 1e-5
RMS_EPS = 1e-5
DEEPNORM_ALPHA = 2.0 ** 0.25
LOG2_E = 1.4426950408889634

SUBLANES = 8
LANES = 128
MXU_COLS = 256
LN_ROWS = 16
STREAM_BUFFERS = 3
VMEM_LIMIT = 56 * 1024 * 1024


def _sigmoid(x):
    return 0.5 * jnp.tanh(0.5 * x) + 0.5


def _silu_of_half(h):
    return h + h * jnp.tanh(h)


def _silu(x):
    return _silu_of_half(0.5 * x)


def _softplus(x):
    return jnp.maximum(x, 0.0) + jnp.log1p(jnp.exp(-jnp.abs(x)))


def _layer_norm_rows(v, g, b):
    mu = jnp.mean(v, axis=-1, keepdims=True)
    d = v - mu
    var = jnp.mean(d * d, axis=-1, keepdims=True)
    return d * lax.rsqrt(var + LN_EPS) * g + b


def _ln_in_kernel(x_ref, meta_ref, g_ref, b_ref, o_ref):
    t = pl.program_id(1)
    g = g_ref[...]
    b = b_ref[...]

    @pl.when(t == 0)
    def _():
        d = o_ref.shape[-1]
        o_ref[0, :PAD, :] = jnp.zeros((PAD, d), o_ref.dtype)
        o_ref[0, PAD:, :] = _layer_norm_rows(meta_ref[...], g, b).astype(o_ref.dtype)

    @pl.when(t > 0)
    def _():
        for r0 in range(0, o_ref.shape[1], LN_ROWS):
            rows = slice(r0, r0 + LN_ROWS)
            o_ref[0, rows, :] = _layer_norm_rows(x_ref[0, rows, :], g, b).astype(o_ref.dtype)


def _ln_in(x, meta, g, b):
    nb, seq, d = x.shape
    nt = seq // CHUNK + 1
    return pl.pallas_call(
        _ln_in_kernel,
        grid=(nb, nt),
        in_specs=[
            pl.BlockSpec((1, CHUNK, d), lambda bi, t: (bi, jnp.maximum(t - 1, 0), 0)),
            pl.BlockSpec((N_META, d), lambda bi, t: (0, 0)),
            pl.BlockSpec((1, d), lambda bi, t: (0, 0)),
            pl.BlockSpec((1, d), lambda bi, t: (0, 0)),
        ],
        out_specs=pl.BlockSpec((1, CHUNK, d), lambda bi, t: (bi, t, 0)),
        out_shape=jax.ShapeDtypeStruct((nb, nt * CHUNK, d), BF16),
        name="ln_in",
    )(x, meta, g.reshape(1, d), b.reshape(1, d))


def _cast_weight_chunk(wc_refs, wb_refs, slot, chunk, kc):
    rows = pl.ds(pl.multiple_of(chunk * kc, kc), kc)
    for wc, wb in zip(wc_refs, wb_refs):
        wb[slot, rows, :] = wc[...].astype(BF16)


def _mm_kernel(*refs, n_w, n_e, epilogue, kc):
    x_ref = refs[0]
    wc_refs = refs[1:1 + n_w]
    e_refs = refs[1 + n_w:1 + n_w + n_e]
    o_ref = refs[1 + n_w + n_e]
    wb_refs = refs[2 + n_w + n_e:]
    jj = pl.program_id(0)
    i = pl.program_id(1)
    n_col_tiles = pl.num_programs(0) - 1

    @pl.when(jj < n_col_tiles)
    def _():
        _cast_weight_chunk(wc_refs, wb_refs, lax.rem(jj, 2), i, kc)

    @pl.when(jj > 0)
    def _():
        slot = lax.rem(jj - 1, 2)
        x = x_ref[...]
        accs = [jnp.dot(x, wb[slot], preferred_element_type=F32) for wb in wb_refs]
        o_ref[...] = epilogue(accs, [e[...] for e in e_refs]).astype(o_ref.dtype)


def _live_row_start(i, tm, live):
    t_rows, row0, seq = live
    tiles_per_seq = seq // tm
    return ((i // tiles_per_seq) * (t_rows // LANES) + row0 // LANES + (i % tiles_per_seq) * (tm // LANES)) * LANES


def _matmul(x, ws, extras, epilogue, out_dtype, tm, tn, name, live=None):
    k = x.shape[1]
    if live is None:
        r = x.shape[0]
    else:
        t_rows, row0, seq = live
        assert x.shape[0] % t_rows == 0 and seq % tm == 0 and tm % LANES == 0
        assert t_rows % LANES == 0 and row0 % LANES == 0
        r = x.shape[0] // t_rows * seq
    n = ws[0][2]
    nj, ni = n // tn, r // tm
    kc = k // ni
    assert k % ni == 0 and kc % 16 == 0 and n % tn == 0 and r % tm == 0
    assert all(c0 % LANES == 0 for _, c0, _ in ws)
    row = lambda jj, i: jnp.where(jj == 0, 0, i)
    col = lambda jj: jnp.maximum(jj - 1, 0)
    if live is None:
        in_specs = [pl.BlockSpec((tm, k), lambda jj, i: (row(jj, i), 0))]
    else:
        in_specs = [pl.BlockSpec((pl.Element(tm), pl.Element(k)),
                                 lambda jj, i: (_live_row_start(row(jj, i), tm, live), 0))]
    in_specs += [pl.BlockSpec((pl.Element(kc), pl.Element(tn)),
                              lambda jj, i, c0=c0: (
                                  i * kc, (c0 // LANES + jnp.minimum(jj, nj - 1) * (tn // LANES)) * LANES))
                 for _, c0, _ in ws]
    args = [x] + [w for w, _, _ in ws]
    for arr, off in extras:
        if arr.shape[0] == 1:
            in_specs.append(pl.BlockSpec((1, tn), lambda jj, i, off=off: (0, col(jj) + off)))
        else:
            in_specs.append(pl.BlockSpec((tm, tn), lambda jj, i, off=off: (row(jj, i), col(jj) + off)))
        args.append(arr)
    return pl.pallas_call(
        functools.partial(_mm_kernel, n_w=len(ws), n_e=len(extras), epilogue=epilogue, kc=kc),
        grid=(nj + 1, ni),
        in_specs=in_specs,
        out_specs=pl.BlockSpec((tm, tn), lambda jj, i: (row(jj, i), col(jj))),
        out_shape=jax.ShapeDtypeStruct((r, n), out_dtype),
        scratch_shapes=[pltpu.VMEM((2, k, tn), BF16) for _ in ws],
        compiler_params=pltpu.CompilerParams(
            dimension_semantics=("arbitrary", "arbitrary"), vmem_limit_bytes=VMEM_LIMIT),
        name=name,
    )(*args)


def _mm_conv_kernel(*refs, n_w, taps, halo, tm, tn, n_row_tiles, n_tiles, tiles_per_seq, act, row_group, kc):
    x_ref = refs[0]
    wc_refs = refs[1:1 + n_w]
    cw_ref, cb_ref, o_ref = refs[1 + n_w:4 + n_w]
    wb_refs = refs[4 + n_w:4 + 2 * n_w]
    vbuf, stage = refs[4 + 2 * n_w:]
    step = pl.program_id(0)
    m = step - n_row_tiles
    stride = tm // SUBLANES
    slabs_per_group = MXU_COLS // LANES
    n_groups = tn // MXU_COLS

    @pl.when(step == 0)
    def _():
        vbuf[...] = jnp.zeros(vbuf.shape, F32)

    @pl.when(step < n_tiles)
    def _():
        _cast_weight_chunk(wc_refs, wb_refs, lax.rem(step // n_row_tiles, 2), lax.rem(step, n_row_tiles), kc)

    @pl.when(m >= 0)
    def _():
        cur = jnp.minimum(m, n_tiles - 1)
        slot = lax.rem(cur // n_row_tiles, 2)
        starts_seq = lax.rem(lax.rem(cur, n_row_tiles), tiles_per_seq) == 0

        def column_group(g, carry):
            col0 = pl.multiple_of(g * MXU_COLS, MXU_COLS)
            for s2 in range(slabs_per_group):
                s = g * slabs_per_group + s2
                lanes = pl.ds(pl.multiple_of(col0 + s2 * LANES, LANES), LANES)
                for i0 in range(0, stride, row_group):
                    accs = [cb_ref[:, lanes]] * row_group
                    for k in range(taps):
                        wk = cw_ref[k * SUBLANES:(k + 1) * SUBLANES, lanes]
                        for ii in range(row_group):
                            start = halo + i0 + ii - (taps - 1) + k
                            accs[ii] = accs[ii] + vbuf[s, pl.ds(start, SUBLANES, stride=stride), :] * wk
                    for ii in range(row_group):
                        stage[s, pl.ds(i0 + ii, SUBLANES, stride=stride), :] = act(accs[ii])
                o_ref[:, lanes] = stage[s].astype(o_ref.dtype)

            x = x_ref[...]
            cols = pl.ds(col0, MXU_COLS)
            if n_w == 2:
                gate = _sigmoid(jnp.dot(x, wb_refs[1][slot, :, cols], preferred_element_type=F32))
                for s2 in range(slabs_per_group):
                    stage[g * slabs_per_group + s2] = gate[:, s2 * LANES:(s2 + 1) * LANES]
            v = jnp.dot(x, wb_refs[0][slot, :, cols], preferred_element_type=F32)
            for s2 in range(slabs_per_group):
                s = g * slabs_per_group + s2
                vbuf[s, 0:halo, :] = jnp.where(starts_seq, 0.0, vbuf[s, tm:tm + halo, :])
                v_slab = v[:, s2 * LANES:(s2 + 1) * LANES]
                vbuf[s, halo:halo + tm, :] = v_slab * stage[s] if n_w == 2 else v_slab
            return carry
        lax.fori_loop(0, n_groups, column_group, 0)


def _matmul_conv(x, ws, conv_w, conv_b, act, tm, tn, tiles_per_seq, name):
    r, k = x.shape
    n_cols = ws[0][2]
    taps = conv_w.shape[0]
    halo = -(-(taps - 1) // SUBLANES) * SUBLANES
    n_row_tiles = r // tm
    n_col_tiles = n_cols // tn
    n_tiles = n_row_tiles * n_col_tiles
    kc = k // n_row_tiles
    assert tm % (2 * SUBLANES) == 0 and tn % MXU_COLS == 0 and k % n_row_tiles == 0 and kc % 16 == 0
    assert all(c0 % LANES == 0 for _, c0, _ in ws)
    row_group = 12
    assert (tm // SUBLANES) % row_group == 0
    rep = lambda a: jnp.broadcast_to(a[:, None, :], (a.shape[0], SUBLANES, n_cols)).reshape(-1, n_cols)

    def cur_tile(step):
        c = jnp.clip(step - n_row_tiles, 0, n_tiles - 1)
        return c % n_row_tiles, c // n_row_tiles

    def lag_tile(step):
        c = jnp.clip(step - n_row_tiles - 1, 0, n_tiles - 1)
        return c % n_row_tiles, c // n_row_tiles

    def chunk(step, c0):
        tile_col = jnp.minimum(step // n_row_tiles, n_col_tiles - 1)
        return (step % n_row_tiles) * kc, (c0 // LANES + tile_col * (tn // LANES)) * LANES

    in_specs = [pl.BlockSpec((tm, k), lambda n: (cur_tile(n)[0], 0))]
    in_specs += [pl.BlockSpec((pl.Element(kc), pl.Element(tn)), lambda n, c0=c0: chunk(n, c0))
                 for _, c0, _ in ws]
    in_specs += [pl.BlockSpec((taps * SUBLANES, tn), lambda n: (0, lag_tile(n)[1])),
                 pl.BlockSpec((SUBLANES, tn), lambda n: (0, lag_tile(n)[1]))]
    return pl.pallas_call(
        functools.partial(_mm_conv_kernel, n_w=len(ws), taps=taps, halo=halo, tm=tm, tn=tn,
                          n_row_tiles=n_row_tiles, n_tiles=n_tiles, tiles_per_seq=tiles_per_seq, act=act,
                          row_group=row_group, kc=kc),
        grid=(n_row_tiles + n_tiles + 1,),
        in_specs=in_specs,
        out_specs=pl.BlockSpec((tm, tn), lambda n: lag_tile(n)),
        out_shape=jax.ShapeDtypeStruct((r, n_cols), BF16),
        scratch_shapes=[pltpu.VMEM((2, k, tn), BF16) for _ in ws] + [
            pltpu.VMEM((tn // LANES, halo + tm, LANES), F32),
            pltpu.VMEM((tn // LANES, tm, LANES), F32)],
        compiler_params=pltpu.CompilerParams(
            dimension_semantics=("arbitrary",), vmem_limit_bytes=VMEM_LIMIT),
        name=name,
    )(x, *[w for w, _, _ in ws], rep(conv_w), rep(conv_b.reshape(1, n_cols)))


def _dt_kernel(x_ref, w_ref, bias_ref, o_ref, ot_ref, *, tm, t_rows):
    i = pl.program_id(0)
    raw = jnp.dot(x_ref[...], w_ref[...], preferred_element_type=F32)
    dt = _softplus(raw + bias_ref[...])
    tiles_per_seq = t_rows // tm
    row_in_seq = lax.rem(i, tiles_per_seq) * tm + lax.broadcasted_iota(jnp.int32, dt.shape, 0)
    dt = jnp.where(row_in_seq < PAD, 0.0, dt)
    o_ref[...] = dt
    ot_ref[...] = dt.T


def _dt_proj(h2, w_dt, dt_bias, tm, t_rows):
    r, k = h2.shape
    nh = w_dt.shape[1]
    assert t_rows % tm == 0 and tm % LANES == 0
    return pl.pallas_call(
        functools.partial(_dt_kernel, tm=tm, t_rows=t_rows),
        grid=(r // tm,),
        in_specs=[
            pl.BlockSpec((tm, k), lambda i: (i, 0)),
            pl.BlockSpec((k, nh), lambda i: (0, 0)),
            pl.BlockSpec((1, nh), lambda i: (0, 0)),
        ],
        out_specs=[pl.BlockSpec((tm, nh), lambda i: (i, 0)),
                   pl.BlockSpec((nh, tm), lambda i: (0, i))],
        out_shape=[jax.ShapeDtypeStruct((r, nh), F32), jax.ShapeDtypeStruct((nh, r), F32)],
        compiler_params=pltpu.CompilerParams(
            dimension_semantics=("arbitrary",), vmem_limit_bytes=VMEM_LIMIT),
        name="dt_proj",
    )(h2, w_dt, dt_bias.reshape(1, nh))


def _ln_act_kernel(cv_ref, sg_ref, g_ref, b_ref, o_ref):
    half_g = 0.5 * g_ref[...]
    half_b = 0.5 * b_ref[...]
    for r0 in range(0, o_ref.shape[0], LN_ROWS):
        rows = slice(r0, r0 + LN_ROWS)
        half_y = _layer_norm_rows(cv_ref[rows, :].astype(F32), half_g, half_b)
        o_ref[rows, :] = (_silu_of_half(half_y) * sg_ref[rows, :].astype(F32)).astype(o_ref.dtype)


def _ln_act(cv, sg, ln_g, ln_b, tl, live):
    r, c = sg.shape
    assert live[2] % tl == 0 and tl % LANES == 0
    vec = pl.BlockSpec((1, c), lambda i: (0, 0))
    tile = pl.BlockSpec((tl, c), lambda i: (i, 0))
    cv_tile = pl.BlockSpec((pl.Element(tl), pl.Element(c)), lambda i: (_live_row_start(i, tl, live), 0))
    return pl.pallas_call(
        _ln_act_kernel,
        grid=(r // tl,),
        in_specs=[cv_tile, tile, vec, vec],
        out_specs=tile,
        out_shape=jax.ShapeDtypeStruct((r, c), BF16),
        compiler_params=pltpu.CompilerParams(
            dimension_semantics=("arbitrary",), vmem_limit_bytes=VMEM_LIMIT),
        name="conv_ln_act",
    )(cv, sg, ln_g.reshape(1, c), ln_b.reshape(1, c))


def _ssd_kernel(xbc_ref, dt_ref, dtt_ref, sz_ref, alog_ref, alogt_ref, dskip_ref, ng_ref,
                o_ref, state, *, inner):
    c = pl.program_id(1)
    q = CHUNK
    n = SSD_STATE
    n_groups = state.shape[0]
    gw = inner // n_groups
    pair_w = 2 * SSD_HEAD_DIM
    n_pairs = gw // pair_w
    heads_per_group = gw // SSD_HEAD_DIM

    @pl.when(c == 0)
    def _():
        state[...] = jnp.zeros(state.shape, F32)

    dt = dt_ref[0]
    dtt = dtt_ref[...]
    a_row = -jnp.exp(alog_ref[...])
    a_col = -jnp.exp(alogt_ref[...])

    rows = lax.broadcasted_iota(jnp.int32, (q, q), 0)
    cols = lax.broadcasted_iota(jnp.int32, (q, q), 1)
    causal = cols <= rows
    tri = jnp.where(causal, 1.0, 0.0).astype(F32)
    tri_t = jnp.where(rows <= cols, 1.0, 0.0).astype(F32)
    a_cs = jnp.dot(tri, dt * a_row, preferred_element_type=F32,
                   precision=lax.Precision.HIGHEST)
    a_cs_t = jnp.dot(dtt * a_col, tri_t, preferred_element_type=F32,
                     precision=lax.Precision.HIGHEST)
    cs2 = a_cs * LOG2_E
    cs2_t = a_cs_t * LOG2_E
    src2_t = cs2_t - jnp.log2(dtt)
    w_state_t = dtt * jnp.exp2(cs2_t[:, q - 1:q] - cs2_t)

    lane = lax.broadcasted_iota(jnp.int32, (q, pair_w), 1)
    first = lane < SSD_HEAD_DIM

    for g in range(n_groups):
        bm_b = xbc_ref[0, :, inner + g * n:inner + (g + 1) * n]
        cm_b = xbc_ref[0, :, inner + (n_groups + g) * n:inner + (n_groups + g + 1) * n]
        cb = lax.dot_general(cm_b, bm_b, (((1,), (1,)), ((), ())), preferred_element_type=F32)
        bm_t = bm_b.astype(F32).T

        y_parts = []
        for j in range(n_pairs):
            h0 = g * heads_per_group + 2 * j
            h1 = h0 + 1
            lo, hi = g * gw + j * pair_w, g * gw + (j + 1) * pair_w
            ac0 = jnp.broadcast_to(cs2[:, h0:h0 + 1], (q, q))
            ac1 = jnp.broadcast_to(cs2[:, h1:h1 + 1], (q, q))
            ac_pair = jnp.where(first, ac0, ac1)
            tot_pair = ac_pair[q - 1:q, :]

            x_b = xbc_ref[0, :, lo:hi]
            zero = jnp.zeros_like(x_b)
            x_bd = jnp.concatenate([jnp.where(first, x_b, zero), jnp.where(first, zero, x_b)], axis=0)

            seg0 = jnp.where(causal, ac0 - src2_t[h0:h0 + 1, :], -jnp.inf)
            seg1 = jnp.where(causal, ac1 - src2_t[h1:h1 + 1, :], -jnp.inf)
            m0 = (cb * jnp.exp2(seg0)).astype(BF16)
            m1 = (cb * jnp.exp2(seg1)).astype(BF16)
            y_diag = jnp.dot(jnp.concatenate([m0, m1], axis=1), x_bd, preferred_element_type=F32)

            st = state[g, :, j * pair_w:(j + 1) * pair_w]
            y_off = jnp.dot(cm_b, st.astype(BF16), preferred_element_type=F32) * jnp.exp2(ac_pair)
            y_parts.append(y_diag + y_off + dskip_ref[:, lo:hi] * x_b.astype(F32))

            w0 = (bm_t * w_state_t[h0:h0 + 1, :]).astype(BF16)
            w1 = (bm_t * w_state_t[h1:h1 + 1, :]).astype(BF16)
            upd = jnp.dot(jnp.concatenate([w0, w1], axis=1), x_bd, preferred_element_type=F32)
            state[g, :, j * pair_w:(j + 1) * pair_w] = st * jnp.exp2(tot_pair) + upd

        y = jnp.concatenate(y_parts, axis=1)
        yz = y * sz_ref[0, :, g * gw:(g + 1) * gw].astype(F32)
        ms = jnp.mean(yz * yz, axis=-1, keepdims=True)
        o_ref[0, :, g * gw:(g + 1) * gw] = (
            yz * lax.rsqrt(ms + RMS_EPS) * ng_ref[:, g * gw:(g + 1) * gw]).astype(o_ref.dtype)


def _ssd_branch(xbc3, dt3, dt_t, sz3, a_log, d_skip, norm_g):
    nb, t_rows, xbc_w = xbc3.shape
    assert sz3.shape[1] == t_rows - CHUNK
    inner = sz3.shape[-1]
    heads = a_log.shape[0]
    nc = t_rows // CHUNK
    dskip_x = jnp.repeat(d_skip, SSD_HEAD_DIM).reshape(1, inner)
    full = lambda shape: pl.BlockSpec(shape, lambda bi, ci: (0,) * len(shape))

    return pl.pallas_call(
        functools.partial(_ssd_kernel, inner=inner),
        grid=(nb, nc),
        in_specs=[
            pl.BlockSpec((1, CHUNK, xbc_w), lambda bi, ci: (bi, ci, 0)),
            pl.BlockSpec((1, CHUNK, heads), lambda bi, ci: (bi, ci, 0)),
            pl.BlockSpec((heads, CHUNK), lambda bi, ci: (0, bi * nc + ci)),
            pl.BlockSpec((1, CHUNK, inner), lambda bi, ci: (bi, jnp.maximum(ci - 1, 0), 0)),
            full((1, heads)), full((heads, 1)), full((1, inner)), full((1, inner)),
        ],
        out_specs=pl.BlockSpec((1, CHUNK, inner), lambda bi, ci: (bi, ci, 0)),
        out_shape=jax.ShapeDtypeStruct((nb, t_rows, inner), BF16),
        scratch_shapes=[pltpu.VMEM((SSD_GROUPS, SSD_STATE, inner // SSD_GROUPS), F32)],
        compiler_params=pltpu.CompilerParams(
            dimension_semantics=("arbitrary", "arbitrary"), vmem_limit_bytes=VMEM_LIMIT),
        name="ssd_branch",
    )(xbc3, dt3, dt_t, sz3, a_log.reshape(1, heads), a_log.reshape(heads, 1), dskip_x,
      norm_g.reshape(1, inner))


def _ln_out_kernel(x_ref, sub_ref, gi_ref, bi_ref, go_ref, bo_ref, o_ref):
    h = _layer_norm_rows(x_ref[0], gi_ref[...], bi_ref[...])
    o_ref[0] = _layer_norm_rows(DEEPNORM_ALPHA * h + sub_ref[0].astype(F32), go_ref[...], bo_ref[...])


def _ln_out(x, sub3, gi, bi, go, bo):
    nb, seq, d = x.shape
    row = lambda a: a.reshape(1, d)
    tile = lambda **kw: pl.BlockSpec((1, CHUNK, d), lambda b_, t: (b_, t, 0), **kw)

    def outer(x_hbm, sub_hbm, gi_ref, bi_ref, go_ref, bo_ref, o_hbm):
        def inner(x_ref, sub_ref, o_ref):
            _ln_out_kernel(x_ref, sub_ref, gi_ref, bi_ref, go_ref, bo_ref, o_ref)
        pltpu.emit_pipeline(
            inner,
            grid=(nb, seq // CHUNK),
            in_specs=[tile(pipeline_mode=pl.Buffered(STREAM_BUFFERS)),
                      tile(pipeline_mode=pl.Buffered(STREAM_BUFFERS))],
            out_specs=[tile()],
        )(x_hbm, sub_hbm, o_hbm)

    vmem = pl.BlockSpec(memory_space=pltpu.VMEM)
    hbm = pl.BlockSpec(memory_space=pl.ANY)
    return pl.pallas_call(
        outer,
        in_specs=[hbm, hbm, vmem, vmem, vmem, vmem],
        out_specs=hbm,
        out_shape=jax.ShapeDtypeStruct((nb, seq, d), F32),
        name="ln_out",
    )(x, sub3, row(gi), row(bi), row(go), row(bo))


def kernel(x, meta_tokens, ln_in_g, ln_in_b, w_in, b_gate, conv_w, conv_b, conv_ln_g, conv_ln_b,
           w_conv_out, ssd_conv_w, ssd_conv_b, dt_bias, a_log, d_skip, ssd_norm_g, w_ssd_out,
           w_out, ln_out_g, ln_out_b):
    nb, seq, d = x.shape
    assert w_in.shape[0] == 1, "single-layer trunk"
    assert seq % CHUNK == 0
    t_rows = seq + CHUNK
    r = nb * t_rows
    inner = w_ssd_out.shape[1]
    heads = a_log.shape[1]
    xbc_w = ssd_conv_w.shape[2]
    tiles_per_seq = 4
    tm = t_rows // tiles_per_seq
    tn = 1024
    assert t_rows % tiles_per_seq == 0 and tm % 16 == 0

    o_val, o_glu, o_gate, o_z = 0, d, 2 * d, 3 * d
    o_xbc = o_z + inner
    o_dt = o_xbc + xbc_w
    o_g = o_dt + heads
    w = w_in[0]

    h3 = _ln_in(x, meta_tokens, ln_in_g, ln_in_b)
    h2 = h3.reshape(r, d)

    cv = _matmul_conv(h2, [(w, o_val, d), (w, o_glu, d)], conv_w[0], conv_b[0],
                      lambda a: a, tm, tn // 2, tiles_per_seq, "in_proj_glu_conv")
    live = (t_rows, CHUNK, seq)
    tml = seq // tiles_per_seq
    sg = _matmul(h2, [(w, o_gate, d)], [], lambda a, e: _silu(a[0]), BF16, tml, tn, "in_proj_gate", live)
    sz = _matmul(h2, [(w, o_z, inner)], [], lambda a, e: _silu(a[0]), BF16, tml, tn, "in_proj_z", live)
    xbc = _matmul_conv(h2, [(w, o_xbc, xbc_w)], 0.5 * ssd_conv_w[0], 0.5 * ssd_conv_b[0],
                       _silu_of_half, tm, tn, tiles_per_seq, "in_proj_xbc_conv")
    dt, dt_t = _dt_proj(h2, w[:, o_dt:o_g].astype(BF16), dt_bias[0], t_rows // 3, t_rows)
    gates = _matmul(h2, [(w, o_g, 2 * d)], [(b_gate[0].reshape(1, 2 * d), 0)],
                    lambda a, e: _sigmoid(a[0] + e[0]), BF16, tml, tn, "in_proj_merge_gates", live)

    u = _ln_act(cv, sg, conv_ln_g[0], conv_ln_b[0], 2 * CHUNK, live)
    yn = _ssd_branch(xbc.reshape(nb, t_rows, xbc_w), dt.reshape(nb, t_rows, heads), dt_t,
                     sz.reshape(nb, seq, inner), a_log[0], d_skip[0], ssd_norm_g[0])

    gated_conv = _matmul(u, [(w_conv_out[0], 0, d)], [(gates, 0)],
                         lambda a, e: a[0] * e[0].astype(F32), BF16, tml, tn, "conv_out_proj")
    merged = _matmul(yn.reshape(r, inner), [(w_ssd_out[0], 0, d)],
                     [(gates, 2 * d // tn), (gated_conv, 0)],
                     lambda a, e: a[0] * e[0].astype(F32) + e[1].astype(F32),
                     BF16, tml // 2, tn // 2, "ssd_out_proj_merge", live)
    sub = _matmul(merged, [(w_out[0], 0, d)], [], lambda a, e: a[0], F32, tml, tn, "out_proj")

    return _ln_out(x, sub.reshape(nb, seq, d), ln_in_g, ln_in_b, ln_out_g[0], ln_out_b[0])
```
